```python
import math
import jax, jax.numpy as jnp
from jax import lax
import numpy as np

D_MODEL = 1024
BATCH = 32
SEQ = 2048
DEPTH = 1

HEAD_DIM = 64
MIX_WIDTH = D_MODEL
A_WIDTH = MIX_WIDTH // 2
A_Q_HEADS = A_WIDTH // HEAD_DIM
A_KV_HEADS = A_Q_HEADS // 4
A_GROUP = A_Q_HEADS // A_KV_HEADS
A_KV_WIDTH = A_KV_HEADS * HEAD_DIM
WINDOW = 128
BLOCK = 128
SPAN = BLOCK + 2 * WINDOW
B_WIDTH = MIX_WIDTH - A_WIDTH
B_HEADS = B_WIDTH // (2 * HEAD_DIM)
B_VDIM = 2 * HEAD_DIM

ROPE_THETA = 10000.0
EPS = 1e-6
SCALE = 1.0 / math.sqrt(HEAD_DIM)
NEG = -1e30

IN_SIZES = (A_WIDTH, A_KV_WIDTH, A_KV_WIDTH, A_WIDTH, B_WIDTH, B_WIDTH, B_WIDTH, B_WIDTH)
IN_WIDTH = sum(IN_SIZES)
IN_SPLITS = tuple(int(v) for v in np.cumsum(IN_SIZES)[:-1])

kernel_name = "hymba_swa_sink_diffattn_adaln_encoder"


def _lambda_init(layer_idx):
    return 0.8 - 0.6 * math.exp(-0.3 * layer_idx)


def _rmsnorm(x, gain):
    xf = x.astype(jnp.float32)
    y = xf * lax.rsqrt(jnp.mean(xf * xf, axis=-1, keepdims=True) + EPS)
    return (y * gain.astype(jnp.float32)).astype(x.dtype)


def _rope_tables(positions):
    inv_freq = 1.0 / (ROPE_THETA ** (jnp.arange(0, HEAD_DIM, 2, dtype=jnp.float32) / HEAD_DIM))
    ang = positions.astype(jnp.float32)[..., None] * inv_freq
    ang = jnp.concatenate([ang, ang], axis=-1)
    return jnp.cos(ang), jnp.sin(ang)


def _rope(x, cos, sin):
    shape = cos.shape[:2] + (1,) * (x.ndim - 3) + (HEAD_DIM,)
    cos = cos.reshape(shape)
    sin = sin.reshape(shape)
    xf = x.astype(jnp.float32)
    x1, x2 = jnp.split(xf, 2, axis=-1)
    rot = jnp.concatenate([-x2, x1], axis=-1)
    return (xf * cos + rot * sin).astype(x.dtype)


def _windowed_gqa_sink(q, k, v, sink):
    B, S = q.shape[0], q.shape[1]
    nblk = S // BLOCK
    kp = jnp.pad(k, ((0, 0), (WINDOW, WINDOW), (0, 0), (0, 0)))
    vp = jnp.pad(v, ((0, 0), (WINDOW, WINDOW), (0, 0), (0, 0)))
    qblocks = jnp.moveaxis(q.reshape(B, nblk, BLOCK, A_KV_HEADS, A_GROUP, HEAD_DIM), 1, 0)
    sink_f = sink.astype(jnp.float32)[None, :, :, None, None]

    def one_block(args):
        i, qb = args
        start = i * BLOCK
        kb = lax.dynamic_slice_in_dim(kp, start, SPAN, axis=1)
        vb = lax.dynamic_slice_in_dim(vp, start, SPAN, axis=1)
        s = jnp.einsum('bqkgd,bjkd->bkgqj', qb, kb).astype(jnp.float32) * SCALE
        qpos = start + jnp.arange(BLOCK)
        kpos = start - WINDOW + jnp.arange(SPAN)
        valid = (kpos[None, :] >= 0) & (kpos[None, :] < S) & (jnp.abs(qpos[:, None] - kpos[None, :]) <= WINDOW)
        s = jnp.where(valid, s, NEG)
        m = jnp.maximum(jnp.max(s, axis=-1, keepdims=True), sink_f)
        p = jnp.exp(s - m)
        p = p / (jnp.sum(p, axis=-1, keepdims=True) + jnp.exp(sink_f - m))
        return jnp.einsum('bkgqj,bjkd->bqkgd', p.astype(vb.dtype), vb)

    o = lax.map(one_block, (jnp.arange(nblk), qblocks))
    return jnp.moveaxis(o, 0, 1).reshape(B, S, A_WIDTH)


def _diff_attention(q, k, v, lam, subln_gain, lambda_init):
    B, S = q.shape[0], q.shape[1]
    nblk = S // BLOCK
    qblocks = jnp.moveaxis(q.reshape(B, nblk, BLOCK, B_HEADS, 2, HEAD_DIM), 1, 0)

    def one_block(qb):
        s = jnp.einsum('bqhcd,bjhcd->bhcqj', qb, k).astype(jnp.float32) * SCALE
        p = jax.nn.softmax(s, axis=-1)
        attn = p[:, :, 0] - lam * p[:, :, 1]
        return jnp.einsum('bhqj,bjhe->bqhe', attn.astype(v.dtype), v)

    o = lax.map(one_block, qblocks)
    o = jnp.moveaxis(o, 0, 1).reshape(B, S, B_HEADS, B_VDIM)
    o = _rmsnorm(o, subln_gain) * (1.0 - lambda_init)
    return o.reshape(B, S, B_WIDTH)


def _layer(x, c, cos, sin, w_ada, b_ada, norm_gain, w_in, q_norm_a, k_norm_a, sink_a,
           q_norm_b, k_norm_b, lambda_q1, lambda_k1, lambda_q2, lambda_k2, subln_gain, w_out,
           lambda_init):
    B, S, _ = x.shape
    mod = jax.nn.silu(c) @ w_ada + b_ada
    shift, scale, gate = jnp.split(mod, 3, axis=-1)
    h = _rmsnorm(x, norm_gain) * (1.0 + scale[:, None, :]) + shift[:, None, :]
    proj = h @ w_in
    qa, ka, va, ga, qb, kb, vb, gb = jnp.split(proj, IN_SPLITS, axis=-1)

    qa = _rope(_rmsnorm(qa.reshape(B, S, A_KV_HEADS, A_GROUP, HEAD_DIM), q_norm_a), cos, sin)
    ka = _rope(_rmsnorm(ka.reshape(B, S, A_KV_HEADS, HEAD_DIM), k_norm_a), cos, sin)
    va = va.reshape(B, S, A_KV_HEADS, HEAD_DIM)
    ya = _windowed_gqa_sink(qa, ka, va, sink_a.reshape(A_KV_HEADS, A_GROUP))

    qb = _rope(_rmsnorm(qb.reshape(B, S, B_HEADS, 2, HEAD_DIM), q_norm_b), cos, sin)
    kb = _rope(_rmsnorm(kb.reshape(B, S, B_HEADS, 2, HEAD_DIM), k_norm_b), cos, sin)
    vb = vb.reshape(B, S, B_HEADS, B_VDIM)
    f32 = jnp.float32
    lam = (jnp.exp(jnp.sum(lambda_q1.astype(f32) * lambda_k1.astype(f32)))
           - jnp.exp(jnp.sum(lambda_q2.astype(f32) * lambda_k2.astype(f32))) + lambda_init)
    yb = _diff_attention(qb, kb, vb, lam, subln_gain, lambda_init)

    y = jnp.concatenate([ya * jax.nn.silu(ga), yb * jax.nn.silu(gb)], axis=-1) @ w_out
    return x + gate[:, None, :] * y


def setup_inputs(seed: int = 0) -> dict:
    key = jax.random.key(seed)
    ks = jax.random.split(key, 20)
    f32 = jnp.float32
    nrm = lambda k, shape, s: jax.random.normal(k, shape, f32) * s
    L = DEPTH
    return {
        "x": nrm(ks[0], (BATCH, SEQ, D_MODEL), 1.0),
        "c": nrm(ks[1], (BATCH, D_MODEL), 1.0),
        "positions": (jnp.arange(SEQ, dtype=jnp.int32)[None, :]
                      + jax.random.randint(ks[2], (BATCH, 1), 0, 4096, dtype=jnp.int32)),
        "w_ada": nrm(ks[3], (L, D_MODEL, 3 * D_MODEL), D_MODEL ** -0.5),
        "b_ada": nrm(ks[4], (L, 3 * D_MODEL), 0.02),
        "norm_gain": 1.0 + nrm(ks[5], (L, D_MODEL), 0.05),
        "w_in": nrm(ks[6], (L, D_MODEL, IN_WIDTH), D_MODEL ** -0.5),
        "q_norm_a": 1.0 + nrm(ks[7], (L, HEAD_DIM), 0.05),
        "k_norm_a": 1.0 + nrm(ks[8], (L, HEAD_DIM), 0.05),
        "sink_a": nrm(ks[9], (L, A_Q_HEADS), 0.5),
        "q_norm_b": 1.0 + nrm(ks[10], (L, HEAD_DIM), 0.05),
        "k_norm_b": 1.0 + nrm(ks[11], (L, HEAD_DIM), 0.05),
        "lambda_q1": nrm(ks[12], (L, HEAD_DIM), 0.1),
        "lambda_k1": nrm(ks[13], (L, HEAD_DIM), 0.1),
        "lambda_q2": nrm(ks[14], (L, HEAD_DIM), 0.1),
        "lambda_k2": nrm(ks[15], (L, HEAD_DIM), 0.1),
        "subln_gain": 1.0 + nrm(ks[16], (L, B_VDIM), 0.05),
        "w_out": nrm(ks[17], (L, MIX_WIDTH, D_MODEL), MIX_WIDTH ** -0.5),
    }


def reference(x, c, positions, w_ada, b_ada, norm_gain, w_in, q_norm_a, k_norm_a, sink_a,
              q_norm_b, k_norm_b, lambda_q1, lambda_k1, lambda_q2, lambda_k2, subln_gain, w_out):
    cos, sin = _rope_tables(positions)
    h = x
    for l in range(DEPTH):
        h = _layer(h, c, cos, sin, w_ada[l], b_ada[l], norm_gain[l], w_in[l], q_norm_a[l],
                   k_norm_a[l], sink_a[l], q_norm_b[l], k_norm_b[l], lambda_q1[l], lambda_k1[l],
                   lambda_q2[l], lambda_k2[l], subln_gain[l], w_out[l], _lambda_init(l))
    return h
```

```python
import functools
import math

import jax
import jax.numpy as jnp
import numpy as np
from jax import lax
from jax.experimental import pallas as pl
from jax.experimental.pallas import tpu as pltpu

F32 = jnp.float32
BF16 = jnp.bfloat16

D_MODEL = 1024
SEQ = 2048
HEAD_DIM = 64
LANES = 128
A_WIDTH = 512
A_KV_WIDTH = 128
B_WIDTH = 512
B_HEADS = 4
WINDOW = 128
BLOCK = 128
SPAN = BLOCK + 2 * WINDOW
ROPE_THETA = 10000.0
EPS = 1e-6
LOG2E = math.log2(math.e)
Q_PRESCALE = LOG2E / math.sqrt(HEAD_DIM)
NEG = -1e30
LAMBDA_INIT = 0.8 - 0.6 * math.exp(-0.3 * 0)

SRC = dict(qa=0, ka=512, va=640, ga=768, qb=1280, kb=1792, vb=2304, gb=2816)
IN_WIDTH = 3328
DST = dict(qa=0, ga=512, qb=1024, kb=1536, vb=2048, gb=2560, ka=3072, va=3200)

VMEM_LIMIT = 48 * 1024 * 1024
TM_IN = 512
TQ_B = 256
TM_OUT = 512


def _nt_dot(a, b):
    return lax.dot_general(a, b, (((1,), (1,)), ((), ())), preferred_element_type=F32)


def _ada_kernel(c_ref, w_ref, b_ref, o_ref):
    c = c_ref[...]
    sc = c * jax.nn.sigmoid(c)
    o_ref[...] = jnp.dot(sc.astype(BF16), w_ref[...].astype(BF16),
                         preferred_element_type=F32) + b_ref[...]


def _ada(c, w_ada, b_ada):
    bsz = c.shape[0]
    n = w_ada.shape[1]
    tn = 1024
    return pl.pallas_call(
        _ada_kernel,
        grid=(n // tn,),
        in_specs=[pl.BlockSpec((bsz, D_MODEL), lambda j: (0, 0)),
                  pl.BlockSpec((D_MODEL, tn), lambda j: (0, j)),
                  pl.BlockSpec((1, tn), lambda j: (0, j))],
        out_specs=pl.BlockSpec((bsz, tn), lambda j: (0, j)),
        out_shape=jax.ShapeDtypeStruct((bsz, n), F32),
        compiler_params=pltpu.CompilerParams(vmem_limit_bytes=VMEM_LIMIT),
        name="ada",
    )(c, w_ada, b_ada.reshape(1, n))


_CHUNKS = (("qk", "qa", 256), ("qk", "ka", 128), ("v", "va", 128), ("gate", "ga", 256),
           ("qk", "qb", 256), ("qk", "kb", 256), ("v", "vb", 256), ("gate", "gb", 256))
_SECTION_WIDTH = dict(qa=512, ka=128, va=128, ga=512, qb=512, kb=512, vb=512, gb=512)


def _inproj_kernel(x_ref, pos_ref, mod_ref, ng_ref, w_ref, invf_ref, cg_ref, ones_ref,
                   o_ref, h_scr):
    x = x_ref[0]
    ms = jnp.mean(x * x, axis=-1, keepdims=True)
    xn = x * lax.rsqrt(ms + EPS)
    shift = mod_ref[0, 0:1, :]
    scale = mod_ref[0, 1:2, :]
    h = (xn * ng_ref[...]) * (1.0 + scale) + shift
    h_scr[...] = h.astype(BF16)

    ang = pos_ref[0].astype(F32) * invf_ref[...]
    cos = jnp.cos(ang)
    sin = jnp.sin(ang)
    lane = lax.broadcasted_iota(jnp.int32, ang.shape, 1)
    first_half = (lane & (HEAD_DIM - 1)) < HEAD_DIM // 2
    sin_a = jnp.where(first_half, -sin, 0.0)
    sin_b = jnp.where(first_half, 0.0, sin)

    for kind, name, w in _CHUNKS:
        for part in range(_SECTION_WIDTH[name] // w):
            src = SRC[name] + part * w
            dst = DST[name] + part * w
            p = jnp.dot(h_scr[...], w_ref[:, src:src + w], preferred_element_type=F32)
            if kind == "v":
                o_ref[0, :, dst:dst + w] = p.astype(BF16)
            elif kind == "gate":
                o_ref[0, :, dst:dst + w] = (p * jax.nn.sigmoid(p)).astype(BF16)
            else:
                ss = jnp.dot((p * p).astype(BF16), ones_ref[0:w, 0:w], preferred_element_type=F32)
                pn = p * lax.rsqrt(ss * (1.0 / HEAD_DIM) + EPS) * cg_ref[:, src:src + w]
                for j in range(w // LANES):
                    t = pn[:, j * LANES:(j + 1) * LANES]
                    r = (t * cos + pltpu.roll(t, LANES - HEAD_DIM // 2, 1) * sin_a
                         + pltpu.roll(t, HEAD_DIM // 2, 1) * sin_b)
                    o_ref[0, :, dst + j * LANES:dst + (j + 1) * LANES] = r.astype(BF16)


def _inproj(x, pos3, mod3, norm_gain, w_in_bf, invf, colgain, ones_blk):
    bsz = x.shape[0]
    return pl.pallas_call(
        _inproj_kernel,
        grid=(bsz, SEQ // TM_IN),
        in_specs=[pl.BlockSpec((1, TM_IN, D_MODEL), lambda b, i: (b, i, 0)),
                  pl.BlockSpec((1, TM_IN, 1), lambda b, i: (b, i, 0)),
                  pl.BlockSpec((1, 3, D_MODEL), lambda b, i: (b, 0, 0)),
                  pl.BlockSpec((1, D_MODEL), lambda b, i: (0, 0)),
                  pl.BlockSpec((D_MODEL, IN_WIDTH), lambda b, i: (0, 0)),
                  pl.BlockSpec((1, LANES), lambda b, i: (0, 0)),
                  pl.BlockSpec((1, IN_WIDTH), lambda b, i: (0, 0)),
                  pl.BlockSpec((256, 256), lambda b, i: (0, 0))],
        out_specs=pl.BlockSpec((1, TM_IN, IN_WIDTH), lambda b, i: (b, i, 0)),
        out_shape=jax.ShapeDtypeStruct((bsz, SEQ, IN_WIDTH), BF16),
        scratch_shapes=[pltpu.VMEM((TM_IN, D_MODEL), BF16)],
        compiler_params=pltpu.CompilerParams(
            dimension_semantics=("parallel", "parallel"), vmem_limit_bytes=VMEM_LIMIT),
        name="inproj",
    )(x, pos3, mod3, norm_gain, w_in_bf, invf, colgain, ones_blk)


def _attn_a_kernel(sink_ref, q_ref, k_ref, v_ref, g_ref, o_ref):
    i = pl.program_id(1)
    start = jnp.clip(i * BLOCK - WINDOW, 0, SEQ - SPAN)
    start = pl.multiple_of(start, BLOCK)
    k = k_ref[0, pl.ds(start, SPAN), :].astype(F32)
    v = v_ref[0, pl.ds(start, SPAN), :].astype(F32)
    lane = lax.broadcasted_iota(jnp.int32, k.shape, 1)
    lo = lane < HEAD_DIM
    k_sw = pltpu.roll(k, HEAD_DIM, 1)
    v_sw = pltpu.roll(v, HEAD_DIM, 1)
    k_even = (jnp.where(lo, k, 0.0).astype(BF16), jnp.where(lo, k_sw, 0.0).astype(BF16))
    k_odd = (jnp.where(lo, 0.0, k_sw).astype(BF16), jnp.where(lo, 0.0, k).astype(BF16))
    v_dup = (jnp.where(lo, v, v_sw).astype(BF16), jnp.where(lo, v_sw, v).astype(BF16))

    qpos = i * BLOCK + (lax.broadcasted_iota(jnp.int32, (2 * BLOCK, SPAN), 0) & (BLOCK - 1))
    kpos = start + lax.broadcasted_iota(jnp.int32, (2 * BLOCK, SPAN), 1)
    valid = jnp.abs(qpos - kpos) <= WINDOW
    lo_out = lax.broadcasted_iota(jnp.int32, (BLOCK, LANES), 1) < HEAD_DIM

    for kv in range(2):
        c0 = 2 * kv * LANES
        q2 = jnp.concatenate([q_ref[0, :, c0:c0 + LANES], q_ref[0, :, c0 + LANES:c0 + 2 * LANES]],
                             axis=0)
        outs = []
        for parity, kmat in ((0, k_even[kv]), (1, k_odd[kv])):
            s = _nt_dot(q2, kmat)
            s = jnp.where(valid, s, NEG)
            row = lax.broadcasted_iota(jnp.int32, (2 * BLOCK, 1), 0)
            sink = jnp.where(row < BLOCK, sink_ref[4 * kv + parity], sink_ref[4 * kv + 2 + parity]) * LOG2E
            m = jnp.maximum(jnp.max(s, axis=-1, keepdims=True), sink)
            p = jnp.exp2(s - m)
            denom = jnp.sum(p, axis=-1, keepdims=True) + jnp.exp2(sink - m)
            o = jnp.dot(p.astype(BF16), v_dup[kv], preferred_element_type=F32)
            outs.append(o / denom)
        for grp in range(2):
            rows = slice(grp * BLOCK, (grp + 1) * BLOCK)
            y = jnp.where(lo_out, outs[0][rows], outs[1][rows])
            cols = slice(c0 + grp * LANES, c0 + (grp + 1) * LANES)
            o_ref[0, :, cols] = (y * g_ref[0, :, cols].astype(F32)).astype(BF16)


def _attn_a(sink, proj):
    bsz = proj.shape[0]
    return pl.pallas_call(
        _attn_a_kernel,
        grid=(bsz, SEQ // BLOCK),
        in_specs=[pl.BlockSpec(memory_space=pltpu.SMEM),
                  pl.BlockSpec((1, BLOCK, A_WIDTH), lambda b, i: (b, i, DST["qa"] // A_WIDTH)),
                  pl.BlockSpec((1, SEQ, A_KV_WIDTH), lambda b, i: (b, 0, DST["ka"] // A_KV_WIDTH)),
                  pl.BlockSpec((1, SEQ, A_KV_WIDTH), lambda b, i: (b, 0, DST["va"] // A_KV_WIDTH)),
                  pl.BlockSpec((1, BLOCK, A_WIDTH), lambda b, i: (b, i, DST["ga"] // A_WIDTH))],
        out_specs=pl.BlockSpec((1, BLOCK, A_WIDTH), lambda b, i: (b, i, 0)),
        out_shape=jax.ShapeDtypeStruct((bsz, SEQ, A_WIDTH), BF16),
        compiler_params=pltpu.CompilerParams(
            dimension_semantics=("parallel", "parallel"), vmem_limit_bytes=VMEM_LIMIT),
        name="attn_a",
    )(sink, proj, proj, proj, proj)


def _attn_b_kernel(lam_ref, q_ref, k_ref, v_ref, g_ref, sg_ref, o_ref):
    lq1, lk1, lq2, lk2 = (lam_ref[r:r + 1, :] for r in range(4))
    lam = (jnp.exp(jnp.sum(lq1 * lk1, axis=-1, keepdims=True))
           - jnp.exp(jnp.sum(lq2 * lk2, axis=-1, keepdims=True)) + LAMBDA_INIT)
    lo = lax.broadcasted_iota(jnp.int32, (TQ_B, LANES), 1) < HEAD_DIM
    for h in range(B_HEADS):
        cols = slice(h * LANES, (h + 1) * LANES)
        qp = q_ref[0, :, cols].astype(F32)
        q01 = jnp.concatenate([jnp.where(lo, qp, 0.0), jnp.where(lo, 0.0, qp)], axis=0).astype(BF16)
        s = _nt_dot(q01, k_ref[0, :, cols])
        m = jnp.max(s, axis=-1, keepdims=True)
        e = jnp.exp2(s - m)
        r = 1.0 / jnp.sum(e, axis=-1, keepdims=True)
        attn = e[:TQ_B] * r[:TQ_B] - e[TQ_B:] * (lam * r[TQ_B:])
        o = jnp.dot(attn.astype(BF16), v_ref[0, :, cols], preferred_element_type=F32)
        on = o * lax.rsqrt(jnp.mean(o * o, axis=-1, keepdims=True) + EPS)
        on = on * sg_ref[...] * (1.0 - LAMBDA_INIT)
        o_ref[0, :, cols] = (on * g_ref[0, :, cols].astype(F32)).astype(BF16)


def _attn_b(lam_params, proj, subln_gain):
    bsz = proj.shape[0]
    return pl.pallas_call(
        _attn_b_kernel,
        grid=(bsz, SEQ // TQ_B),
        in_specs=[pl.BlockSpec((4, HEAD_DIM), lambda b, i: (0, 0)),
                  pl.BlockSpec((1, TQ_B, B_WIDTH), lambda b, i: (b, i, DST["qb"] // B_WIDTH)),
                  pl.BlockSpec((1, SEQ, B_WIDTH), lambda b, i: (b, 0, DST["kb"] // B_WIDTH)),
                  pl.BlockSpec((1, SEQ, B_WIDTH), lambda b, i: (b, 0, DST["vb"] // B_WIDTH)),
                  pl.BlockSpec((1, TQ_B, B_WIDTH), lambda b, i: (b, i, DST["gb"] // B_WIDTH)),
                  pl.BlockSpec((1, LANES), lambda b, i: (0, 0))],
        out_specs=pl.BlockSpec((1, TQ_B, B_WIDTH), lambda b, i: (b, i, 0)),
        out_shape=jax.ShapeDtypeStruct((bsz, SEQ, B_WIDTH), BF16),
        compiler_params=pltpu.CompilerParams(
            dimension_semantics=("parallel", "parallel"), vmem_limit_bytes=VMEM_LIMIT),
        name="attn_b",
    )(lam_params, proj, proj, proj, proj, subln_gain)


def _outproj_kernel(x_ref, ya_ref, yb_ref, w_ref, mod_ref, o_ref):
    y = (jnp.dot(ya_ref[0], w_ref[0:A_WIDTH, :], preferred_element_type=F32)
         + jnp.dot(yb_ref[0], w_ref[A_WIDTH:, :], preferred_element_type=F32))
    o_ref[0] = x_ref[0] + mod_ref[0, 2:3, :] * y


def _outproj(x, ya, yb, w_out_bf, mod3):
    bsz = x.shape[0]
    return pl.pallas_call(
        _outproj_kernel,
        grid=(bsz, SEQ // TM_OUT),
        in_specs=[pl.BlockSpec((1, TM_OUT, D_MODEL), lambda b, i: (b, i, 0)),
                  pl.BlockSpec((1, TM_OUT, A_WIDTH), lambda b, i: (b, i, 0)),
                  pl.BlockSpec((1, TM_OUT, B_WIDTH), lambda b, i: (b, i, 0)),
                  pl.BlockSpec((A_WIDTH + B_WIDTH, D_MODEL), lambda b, i: (0, 0)),
                  pl.BlockSpec((1, 3, D_MODEL), lambda b, i: (b, 0, 0))],
        out_specs=pl.BlockSpec((1, TM_OUT, D_MODEL), lambda b, i: (b, i, 0)),
        out_shape=jax.ShapeDtypeStruct((bsz, SEQ, D_MODEL), F32),
        compiler_params=pltpu.CompilerParams(
            dimension_semantics=("parallel", "parallel"), vmem_limit_bytes=VMEM_LIMIT),
        name="outproj",
    )(x, ya, yb, w_out_bf, mod3)


def _column_gains(q_norm_a, k_norm_a, q_norm_b, k_norm_b):
    g = jnp.ones((IN_WIDTH,), F32)
    g = g.at[SRC["qa"]:SRC["qa"] + 512].set(jnp.tile(q_norm_a.astype(F32), 8) * Q_PRESCALE)
    g = g.at[SRC["ka"]:SRC["ka"] + 128].set(jnp.tile(k_norm_a.astype(F32), 2))
    g = g.at[SRC["qb"]:SRC["qb"] + 512].set(jnp.tile(q_norm_b.astype(F32), 8) * Q_PRESCALE)
    g = g.at[SRC["kb"]:SRC["kb"] + 512].set(jnp.tile(k_norm_b.astype(F32), 8))
    return g.reshape(1, IN_WIDTH)


def kernel(x, c, positions, w_ada, b_ada, norm_gain, w_in, q_norm_a, k_norm_a, sink_a, q_norm_b,
           k_norm_b, lambda_q1, lambda_k1, lambda_q2, lambda_k2, subln_gain, w_out):
    assert w_ada.shape[0] == 1, "single-layer trunk"
    bsz = x.shape[0]
    inv_freq = 1.0 / (ROPE_THETA ** (jnp.arange(0, HEAD_DIM, 2, dtype=F32) / HEAD_DIM))
    invf = jnp.tile(inv_freq, LANES // (HEAD_DIM // 2)).reshape(1, LANES)
    ones_blk = jnp.asarray(np.kron(np.eye(4), np.ones((HEAD_DIM, HEAD_DIM))), dtype=BF16)
    colgain = _column_gains(q_norm_a[0], k_norm_a[0], q_norm_b[0], k_norm_b[0])
    lam_params = jnp.concatenate([lambda_q1, lambda_k1, lambda_q2, lambda_k2], axis=0).astype(F32)

    mod3 = _ada(c, w_ada[0], b_ada[0]).reshape(bsz, 3, D_MODEL)
    proj = _inproj(x, positions.reshape(bsz, SEQ, 1), mod3, norm_gain, w_in[0].astype(BF16),
                   invf, colgain, ones_blk)
    ya = _attn_a(sink_a[0].astype(F32), proj)
    yb = _attn_b(lam_params, proj, subln_gain)
    return _outproj(x, ya, yb, w_out[0].astype(BF16), mod3)
```

```python
import functools
import math

import jax
import jax.numpy as jnp
import numpy as np
from jax import lax
from jax.experimental import pallas as pl
from jax.experimental.pallas import tpu as pltpu

F32 = jnp.float32
BF16 = jnp.bfloat16

D_MODEL = 1024
SEQ = 2048
HEAD_DIM = 64
LANES = 128
A_WIDTH = 512
A_KV_WIDTH = 128
B_WIDTH = 512
B_HEADS = 4
WINDOW = 128
BLOCK = 128
SPAN = BLOCK + 2 * WINDOW
ROPE_THETA = 10000.0
EPS = 1e-6
LOG2E = math.log2(math.e)
Q_PRESCALE = LOG2E / math.sqrt(HEAD_DIM)
NEG = -1e30
LAMBDA_INIT = 0.8 - 0.6 * math.exp(-0.3 * 0)

SRC = dict(qa=0, ka=512, va=640, ga=768, qb=1280, kb=1792, vb=2304, gb=2816)
IN_WIDTH = 3328
DST = dict(qa=0, ga=512, qb=1024, kb=1536, vb=2048, gb=2560, ka=3072, va=3200)

VMEM_LIMIT = 48 * 1024 * 1024
TM_IN = 512
TQ_B = 256
TM_OUT = 512


def _nt_dot(a, b):
    return lax.dot_general(a, b, (((1,), (1,)), ((), ())), preferred_element_type=F32)


def _ada_kernel(c_ref, w_ref, b_ref, o_ref):
    c = c_ref[...]
    sc = c * jax.nn.sigmoid(c)
    o_ref[...] = jnp.dot(sc.astype(BF16), w_ref[...].astype(BF16),
                         preferred_element_type=F32) + b_ref[...]


def _ada(c, w_ada, b_ada):
    bsz = c.shape[0]
    n = w_ada.shape[1]
    tn = 1024
    return pl.pallas_call(
        _ada_kernel,
        grid=(n // tn,),
        in_specs=[pl.BlockSpec((bsz, D_MODEL), lambda j: (0, 0)),
                  pl.BlockSpec((D_MODEL, tn), lambda j: (0, j)),
                  pl.BlockSpec((1, tn), lambda j: (0, j))],
        out_specs=pl.BlockSpec((bsz, tn), lambda j: (0, j)),
        out_shape=jax.ShapeDtypeStruct((bsz, n), F32),
        compiler_params=pltpu.CompilerParams(vmem_limit_bytes=VMEM_LIMIT),
        name="ada",
    )(c, w_ada, b_ada.reshape(1, n))


_SECTION_KIND = dict(qa="qk", ka="qk", va="v", ga="gate", qb="qk", kb="qk", vb="v", gb="gate")
_SECTION_WIDTH = dict(qa=512, ka=128, va=128, ga=512, qb=512, kb=512, vb=512, gb=512)
MM_WIDTH = 1024
EPI_WIDTH = 256


def _matmul_chunks():
    pieces = []
    for name in sorted(SRC, key=SRC.get):
        for off in range(0, _SECTION_WIDTH[name], EPI_WIDTH):
            w = min(EPI_WIDTH, _SECTION_WIDTH[name] - off)
            pieces.append((_SECTION_KIND[name], SRC[name] + off, w, DST[name] + off))
    chunks = []
    for kind, src, w, dst in pieces:
        if chunks and src + w - chunks[-1][0] <= MM_WIDTH:
            chunks[-1][2].append((kind, src - chunks[-1][0], w, dst))
            chunks[-1][1] = src + w - chunks[-1][0]
        else:
            chunks.append([src, w, [(kind, 0, w, dst)]])
    return tuple((s, w, tuple(p)) for s, w, p in chunks)


_MATMUL_CHUNKS = _matmul_chunks()


def _rope_kernel(pos_ref, invf_ref, cos_ref, sin_ref):
    ang = invf_ref[...] * pos_ref[0].astype(F32)
    cos = jnp.cos(ang)
    sin = jnp.sin(ang)
    cos4 = jnp.concatenate([cos, cos, cos, cos], axis=0)
    sin4 = jnp.concatenate([-sin, sin, -sin, sin], axis=0)
    cos_ref[0] = cos4.T
    sin_ref[0] = sin4.T


def _rope_tables(positions, inv_freq):
    bsz = positions.shape[0]
    nfreq = HEAD_DIM // 2
    return pl.pallas_call(
        _rope_kernel,
        grid=(bsz,),
        in_specs=[pl.BlockSpec((1, 1, SEQ), lambda b: (b, 0, 0)),
                  pl.BlockSpec((nfreq, 1), lambda b: (0, 0))],
        out_specs=[pl.BlockSpec((1, SEQ, LANES), lambda b: (b, 0, 0)),
                   pl.BlockSpec((1, SEQ, LANES), lambda b: (b, 0, 0))],
        out_shape=[jax.ShapeDtypeStruct((bsz, SEQ, LANES), F32)] * 2,
        compiler_params=pltpu.CompilerParams(
            dimension_semantics=("parallel",), vmem_limit_bytes=VMEM_LIMIT),
        name="rope",
    )(positions.reshape(bsz, 1, SEQ), inv_freq.reshape(nfreq, 1))


def _inproj_kernel(x_ref, cos_ref, sin_ref, mod_ref, ng_ref, w_ref, cg_ref, ones_ref,
                   o_ref, h_scr):
    x = x_ref[0]
    ms = jnp.mean(x * x, axis=-1, keepdims=True)
    xn = x * lax.rsqrt(ms + EPS)
    shift = mod_ref[0, 0:1, :]
    scale = mod_ref[0, 1:2, :]
    h = (xn * ng_ref[...]) * (1.0 + scale) + shift
    h_scr[...] = h.astype(BF16)

    cos = cos_ref[0]
    sin = sin_ref[0]
    lane = lax.broadcasted_iota(jnp.int32, cos.shape, 1)
    first_half = (lane & (HEAD_DIM - 1)) < HEAD_DIM // 2

    for src0, width, pieces in _MATMUL_CHUNKS:
        pm = jnp.dot(h_scr[...], w_ref[:, src0:src0 + width], preferred_element_type=F32)
        for kind, off, w, dst in pieces:
            src = src0 + off
            p = pm[:, off:off + w]
            if kind == "v":
                o_ref[0, :, dst:dst + w] = p.astype(BF16)
            elif kind == "gate":
                o_ref[0, :, dst:dst + w] = (p * jax.nn.sigmoid(p)).astype(BF16)
            else:
                ss = jnp.dot((p * p).astype(BF16), ones_ref[0:w, 0:w], preferred_element_type=F32)
                pn = p * lax.rsqrt(ss * (1.0 / HEAD_DIM) + EPS) * cg_ref[:, src:src + w]
                for j in range(w // LANES):
                    t = pn[:, j * LANES:(j + 1) * LANES]
                    rot = jnp.where(first_half, pltpu.roll(t, LANES - HEAD_DIM // 2, 1),
                                    pltpu.roll(t, HEAD_DIM // 2, 1))
                    r = t * cos + rot * sin
                    o_ref[0, :, dst + j * LANES:dst + (j + 1) * LANES] = r.astype(BF16)


def _inproj(x, cos_t, sin_t, mod3, norm_gain, w_in_bf, colgain, ones_blk):
    bsz = x.shape[0]
    return pl.pallas_call(
        _inproj_kernel,
        grid=(bsz, SEQ // TM_IN),
        in_specs=[pl.BlockSpec((1, TM_IN, D_MODEL), lambda b, i: (b, i, 0)),
                  pl.BlockSpec((1, TM_IN, LANES), lambda b, i: (b, i, 0)),
                  pl.BlockSpec((1, TM_IN, LANES), lambda b, i: (b, i, 0)),
                  pl.BlockSpec((1, 3, D_MODEL), lambda b, i: (b, 0, 0)),
                  pl.BlockSpec((1, D_MODEL), lambda b, i: (0, 0)),
                  pl.BlockSpec((D_MODEL, IN_WIDTH), lambda b, i: (0, 0)),
                  pl.BlockSpec((1, IN_WIDTH), lambda b, i: (0, 0)),
                  pl.BlockSpec((256, 256), lambda b, i: (0, 0))],
        out_specs=pl.BlockSpec((1, TM_IN, IN_WIDTH), lambda b, i: (b, i, 0)),
        out_shape=jax.ShapeDtypeStruct((bsz, SEQ, IN_WIDTH), BF16),
        scratch_shapes=[pltpu.VMEM((TM_IN, D_MODEL), BF16)],
        compiler_params=pltpu.CompilerParams(
            dimension_semantics=("parallel", "parallel"), vmem_limit_bytes=VMEM_LIMIT),
        name="inproj",
    )(x, cos_t, sin_t, mod3, norm_gain, w_in_bf, colgain, ones_blk)


def _attn_a_kernel(sink_ref, q_ref, k_ref, v_ref, g_ref, o_ref):
    i = pl.program_id(1)
    start = jnp.clip(i * BLOCK - WINDOW, 0, SEQ - SPAN)
    start = pl.multiple_of(start, BLOCK)
    k = k_ref[0, pl.ds(start, SPAN), :].astype(F32)
    v = v_ref[0, pl.ds(start, SPAN), :].astype(F32)
    lane = lax.broadcasted_iota(jnp.int32, k.shape, 1)
    lo = lane < HEAD_DIM
    k_sw = pltpu.roll(k, HEAD_DIM, 1)
    v_sw = pltpu.roll(v, HEAD_DIM, 1)
    k_even = (jnp.where(lo, k, 0.0).astype(BF16), jnp.where(lo, k_sw, 0.0).astype(BF16))
    k_odd = (jnp.where(lo, 0.0, k_sw).astype(BF16), jnp.where(lo, 0.0, k).astype(BF16))
    v_dup = (jnp.where(lo, v, v_sw).astype(BF16), jnp.where(lo, v_sw, v).astype(BF16))

    qpos = i * BLOCK + (lax.broadcasted_iota(jnp.int32, (2 * BLOCK, SPAN), 0) & (BLOCK - 1))
    kpos = start + lax.broadcasted_iota(jnp.int32, (2 * BLOCK, SPAN), 1)
    valid = jnp.abs(qpos - kpos) <= WINDOW
    lo_out = lax.broadcasted_iota(jnp.int32, (BLOCK, LANES), 1) < HEAD_DIM

    for kv in range(2):
        c0 = 2 * kv * LANES
        q2 = jnp.concatenate([q_ref[0, :, c0:c0 + LANES], q_ref[0, :, c0 + LANES:c0 + 2 * LANES]],
                             axis=0)
        outs = []
        for parity, kmat in ((0, k_even[kv]), (1, k_odd[kv])):
            s = _nt_dot(q2, kmat)
            s = jnp.where(valid, s, NEG)
            row = lax.broadcasted_iota(jnp.int32, (2 * BLOCK, 1), 0)
            sink = jnp.where(row < BLOCK, sink_ref[4 * kv + parity], sink_ref[4 * kv + 2 + parity]) * LOG2E
            m = jnp.maximum(jnp.max(s, axis=-1, keepdims=True), sink)
            p = jnp.exp2(s - m)
            denom = jnp.sum(p, axis=-1, keepdims=True) + jnp.exp2(sink - m)
            o = jnp.dot(p.astype(BF16), v_dup[kv], preferred_element_type=F32)
            outs.append(o / denom)
        for grp in range(2):
            rows = slice(grp * BLOCK, (grp + 1) * BLOCK)
            y = jnp.where(lo_out, outs[0][rows], outs[1][rows])
            cols = slice(c0 + grp * LANES, c0 + (grp + 1) * LANES)
            o_ref[0, :, cols] = (y * g_ref[0, :, cols].astype(F32)).astype(BF16)


def _attn_a(sink, proj):
    bsz = proj.shape[0]
    return pl.pallas_call(
        _attn_a_kernel,
        grid=(bsz, SEQ // BLOCK),
        in_specs=[pl.BlockSpec(memory_space=pltpu.SMEM),
                  pl.BlockSpec((1, BLOCK, A_WIDTH), lambda b, i: (b, i, DST["qa"] // A_WIDTH)),
                  pl.BlockSpec((1, SEQ, A_KV_WIDTH), lambda b, i: (b, 0, DST["ka"] // A_KV_WIDTH)),
                  pl.BlockSpec((1, SEQ, A_KV_WIDTH), lambda b, i: (b, 0, DST["va"] // A_KV_WIDTH)),
                  pl.BlockSpec((1, BLOCK, A_WIDTH), lambda b, i: (b, i, DST["ga"] // A_WIDTH))],
        out_specs=pl.BlockSpec((1, BLOCK, A_WIDTH), lambda b, i: (b, i, 0)),
        out_shape=jax.ShapeDtypeStruct((bsz, SEQ, A_WIDTH), BF16),
        compiler_params=pltpu.CompilerParams(
            dimension_semantics=("parallel", "parallel"), vmem_limit_bytes=VMEM_LIMIT),
        name="attn_a",
    )(sink, proj, proj, proj, proj)


def _attn_b_kernel(lam_ref, q_ref, k_ref, v_ref, g_ref, sg_ref, o_ref):
    lq1, lk1, lq2, lk2 = (lam_ref[r:r + 1, :] for r in range(4))
    lam = (jnp.exp(jnp.sum(lq1 * lk1, axis=-1, keepdims=True))
           - jnp.exp(jnp.sum(lq2 * lk2, axis=-1, keepdims=True)) + LAMBDA_INIT)
    lo = lax.broadcasted_iota(jnp.int32, (TQ_B, LANES), 1) < HEAD_DIM
    def scores(h):
        cols = slice(h * LANES, (h + 1) * LANES)
        qp = q_ref[0, :, cols].astype(F32)
        q01 = jnp.concatenate([jnp.where(lo, qp, 0.0), jnp.where(lo, 0.0, qp)], axis=0).astype(BF16)
        return _nt_dot(q01, k_ref[0, :, cols])

    s_next = scores(0)
    for h in range(B_HEADS):
        cols = slice(h * LANES, (h + 1) * LANES)
        s = s_next
        if h + 1 < B_HEADS:
            s_next = scores(h + 1)
        m = jnp.max(s, axis=-1, keepdims=True)
        e = jnp.exp2(s - m)
        l = jnp.sum(e, axis=-1, keepdims=True)
        attn = e[:TQ_B] - e[TQ_B:] * (lam * l[:TQ_B] / l[TQ_B:])
        o = jnp.dot(attn.astype(BF16), v_ref[0, :, cols], preferred_element_type=F32)
        o = o / l[:TQ_B]
        on = o * lax.rsqrt(jnp.mean(o * o, axis=-1, keepdims=True) + EPS)
        on = on * sg_ref[...] * (1.0 - LAMBDA_INIT)
        o_ref[0, :, cols] = (on * g_ref[0, :, cols].astype(F32)).astype(BF16)


def _attn_b(lam_params, proj, subln_gain):
    bsz = proj.shape[0]
    return pl.pallas_call(
        _attn_b_kernel,
        grid=(bsz, SEQ // TQ_B),
        in_specs=[pl.BlockSpec((4, HEAD_DIM), lambda b, i: (0, 0)),
                  pl.BlockSpec((1, TQ_B, B_WIDTH), lambda b, i: (b, i, DST["qb"] // B_WIDTH)),
                  pl.BlockSpec((1, SEQ, B_WIDTH), lambda b, i: (b, 0, DST["kb"] // B_WIDTH)),
                  pl.BlockSpec((1, SEQ, B_WIDTH), lambda b, i: (b, 0, DST["vb"] // B_WIDTH)),
                  pl.BlockSpec((1, TQ_B, B_WIDTH), lambda b, i: (b, i, DST["gb"] // B_WIDTH)),
                  pl.BlockSpec((1, LANES), lambda b, i: (0, 0))],
        out_specs=pl.BlockSpec((1, TQ_B, B_WIDTH), lambda b, i: (b, i, 0)),
        out_shape=jax.ShapeDtypeStruct((bsz, SEQ, B_WIDTH), BF16),
        compiler_params=pltpu.CompilerParams(
            dimension_semantics=("parallel", "parallel"), vmem_limit_bytes=VMEM_LIMIT),
        name="attn_b",
    )(lam_params, proj, proj, proj, proj, subln_gain)


def _outproj_kernel(x_ref, ya_ref, yb_ref, w_ref, mod_ref, o_ref):
    y = (jnp.dot(ya_ref[0], w_ref[0:A_WIDTH, :], preferred_element_type=F32)
         + jnp.dot(yb_ref[0], w_ref[A_WIDTH:, :], preferred_element_type=F32))
    o_ref[0] = x_ref[0] + mod_ref[0, 2:3, :] * y


def _outproj(x, ya, yb, w_out_bf, mod3):
    bsz = x.shape[0]
    return pl.pallas_call(
        _outproj_kernel,
        grid=(bsz, SEQ // TM_OUT),
        in_specs=[pl.BlockSpec((1, TM_OUT, D_MODEL), lambda b, i: (b, i, 0)),
                  pl.BlockSpec((1, TM_OUT, A_WIDTH), lambda b, i: (b, i, 0)),
                  pl.BlockSpec((1, TM_OUT, B_WIDTH), lambda b, i: (b, i, 0)),
                  pl.BlockSpec((A_WIDTH + B_WIDTH, D_MODEL), lambda b, i: (0, 0)),
                  pl.BlockSpec((1, 3, D_MODEL), lambda b, i: (b, 0, 0))],
        out_specs=pl.BlockSpec((1, TM_OUT, D_MODEL), lambda b, i: (b, i, 0)),
        out_shape=jax.ShapeDtypeStruct((bsz, SEQ, D_MODEL), F32),
        compiler_params=pltpu.CompilerParams(
            dimension_semantics=("parallel", "parallel"), vmem_limit_bytes=VMEM_LIMIT),
        name="outproj",
    )(x, ya, yb, w_out_bf, mod3)


def _column_gains(q_norm_a, k_norm_a, q_norm_b, k_norm_b):
    g = jnp.ones((IN_WIDTH,), F32)
    g = g.at[SRC["qa"]:SRC["qa"] + 512].set(jnp.tile(q_norm_a.astype(F32), 8) * Q_PRESCALE)
    g = g.at[SRC["ka"]:SRC["ka"] + 128].set(jnp.tile(k_norm_a.astype(F32), 2))
    g = g.at[SRC["qb"]:SRC["qb"] + 512].set(jnp.tile(q_norm_b.astype(F32), 8) * Q_PRESCALE)
    g = g.at[SRC["kb"]:SRC["kb"] + 512].set(jnp.tile(k_norm_b.astype(F32), 8))
    return g.reshape(1, IN_WIDTH)


def kernel(x, c, positions, w_ada, b_ada, norm_gain, w_in, q_norm_a, k_norm_a, sink_a, q_norm_b,
           k_norm_b, lambda_q1, lambda_k1, lambda_q2, lambda_k2, subln_gain, w_out):
    assert w_ada.shape[0] == 1, "single-layer trunk"
    bsz = x.shape[0]
    inv_freq = 1.0 / (ROPE_THETA ** (jnp.arange(0, HEAD_DIM, 2, dtype=F32) / HEAD_DIM))
    ones_blk = jnp.asarray(np.kron(np.eye(4), np.ones((HEAD_DIM, HEAD_DIM))), dtype=BF16)
    colgain = _column_gains(q_norm_a[0], k_norm_a[0], q_norm_b[0], k_norm_b[0])
    lam_params = jnp.concatenate([lambda_q1, lambda_k1, lambda_q2, lambda_k2], axis=0).astype(F32)

    mod3 = _ada(c, w_ada[0], b_ada[0]).reshape(bsz, 3, D_MODEL)
    cos_t, sin_t = _rope_tables(positions, inv_freq)
    proj = _inproj(x, cos_t, sin_t, mod3, norm_gain, w_in[0].astype(BF16), colgain, ones_blk)
    ya = _attn_a(sink_a[0].astype(F32), proj)
    yb = _attn_b(lam_params, proj, subln_gain)
    return _outproj(x, ya, yb, w_out[0].astype(BF16), mod3)
```

```python
import functools
import math

import jax
import jax.numpy as jnp
import numpy as np
from jax import lax
from jax.experimental import pallas as pl
from jax.experimental.pallas import tpu as pltpu

F32 = jnp.float32
BF16 = jnp.bfloat16

D_MODEL = 1024
SEQ = 2048
HEAD_DIM = 64
LANES = 128
A_WIDTH = 512
A_KV_WIDTH = 128
B_WIDTH = 512
B_HEADS = 4
WINDOW = 128
BLOCK = 128
SPAN = BLOCK + 2 * WINDOW
ROPE_THETA = 10000.0
EPS = 1e-6
LOG2E = math.log2(math.e)
Q_PRESCALE = LOG2E / math.sqrt(HEAD_DIM)
NEG = -1e30
LAMBDA_INIT = 0.8 - 0.6 * math.exp(-0.3 * 0)

SRC = dict(qa=0, ka=512, va=640, ga=768, qb=1280, kb=1792, vb=2304, gb=2816)
IN_WIDTH = 3328
DST = dict(qa=0, ga=512, qb=1024, kb=1536, vb=2048, gb=2560, ka=3072, va=3584)
KA_VARIANTS = 4
VA_VARIANTS = 2
PROJ_WIDTH = 3840

VMEM_LIMIT = 48 * 1024 * 1024
TM_IN = 512
NB_A = 4
TQ_B = 256
TM_OUT = 512


def _nt_dot(a, b):
    return lax.dot_general(a, b, (((1,), (1,)), ((), ())), preferred_element_type=F32)


def _ada_kernel(c_ref, w_ref, b_ref, o_ref):
    c = c_ref[...]
    sc = c * jax.nn.sigmoid(c)
    o_ref[...] = jnp.dot(sc.astype(BF16), w_ref[...].astype(BF16),
                         preferred_element_type=F32) + b_ref[...]


def _ada(c, w_ada, b_ada):
    bsz = c.shape[0]
    n = w_ada.shape[1]
    tn = 1024
    return pl.pallas_call(
        _ada_kernel,
        grid=(n // tn,),
        in_specs=[pl.BlockSpec((bsz, D_MODEL), lambda j: (0, 0)),
                  pl.BlockSpec((D_MODEL, tn), lambda j: (0, j)),
                  pl.BlockSpec((1, tn), lambda j: (0, j))],
        out_specs=pl.BlockSpec((bsz, tn), lambda j: (0, j)),
        out_shape=jax.ShapeDtypeStruct((bsz, n), F32),
        compiler_params=pltpu.CompilerParams(vmem_limit_bytes=VMEM_LIMIT),
        name="ada",
    )(c, w_ada, b_ada.reshape(1, n))


_SECTION_KIND = dict(qa="qk", ka="ka", va="va", ga="gate", qb="qk", kb="qk", vb="v", gb="gate")
_SECTION_WIDTH = dict(qa=512, ka=128, va=128, ga=512, qb=512, kb=512, vb=512, gb=512)
MM_WIDTH = 1024
EPI_WIDTH = 256


def _matmul_chunks():
    pieces = []
    for name in sorted(SRC, key=SRC.get):
        for off in range(0, _SECTION_WIDTH[name], EPI_WIDTH):
            w = min(EPI_WIDTH, _SECTION_WIDTH[name] - off)
            pieces.append((_SECTION_KIND[name], SRC[name] + off, w, DST[name] + off))
    chunks = []
    for kind, src, w, dst in pieces:
        if chunks and src + w - chunks[-1][0] <= MM_WIDTH:
            chunks[-1][2].append((kind, src - chunks[-1][0], w, dst))
            chunks[-1][1] = src + w - chunks[-1][0]
        else:
            chunks.append([src, w, [(kind, 0, w, dst)]])
    return tuple((s, w, tuple(p)) for s, w, p in chunks)


_MATMUL_CHUNKS = _matmul_chunks()


def _rope_kernel(pos_ref, invf_ref, cos_ref, sin_ref):
    ang = invf_ref[...] * pos_ref[0].astype(F32)
    cos = jnp.cos(ang)
    sin = jnp.sin(ang)
    cos4 = jnp.concatenate([cos, cos, cos, cos], axis=0)
    sin4 = jnp.concatenate([-sin, sin, -sin, sin], axis=0)
    cos_ref[0] = cos4.T
    sin_ref[0] = sin4.T


def _rope_tables(positions, inv_freq):
    bsz = positions.shape[0]
    nfreq = HEAD_DIM // 2
    return pl.pallas_call(
        _rope_kernel,
        grid=(bsz,),
        in_specs=[pl.BlockSpec((1, 1, SEQ), lambda b: (b, 0, 0)),
                  pl.BlockSpec((nfreq, 1), lambda b: (0, 0))],
        out_specs=[pl.BlockSpec((1, SEQ, LANES), lambda b: (b, 0, 0)),
                   pl.BlockSpec((1, SEQ, LANES), lambda b: (b, 0, 0))],
        out_shape=[jax.ShapeDtypeStruct((bsz, SEQ, LANES), F32)] * 2,
        compiler_params=pltpu.CompilerParams(
            dimension_semantics=("parallel",), vmem_limit_bytes=VMEM_LIMIT),
        name="rope",
    )(positions.reshape(bsz, 1, SEQ), inv_freq.reshape(nfreq, 1))


def _inproj_kernel(x_ref, cos_ref, sin_ref, mod_ref, ng_ref, w_ref, cg_ref, ones_ref,
                   o_ref, h_scr):
    x = x_ref[0]
    ms = jnp.mean(x * x, axis=-1, keepdims=True)
    xn = x * lax.rsqrt(ms + EPS)
    shift = mod_ref[0, 0:1, :]
    scale = mod_ref[0, 1:2, :]
    h = (xn * ng_ref[...]) * (1.0 + scale) + shift
    h_scr[...] = h.astype(BF16)

    cos = cos_ref[0]
    sin = sin_ref[0]
    lane = lax.broadcasted_iota(jnp.int32, cos.shape, 1)
    first_half = (lane & (HEAD_DIM - 1)) < HEAD_DIM // 2
    low_head = lane < HEAD_DIM

    for src0, width, pieces in _MATMUL_CHUNKS:
        pm = jnp.dot(h_scr[...], w_ref[:, src0:src0 + width], preferred_element_type=F32)
        for kind, off, w, dst in pieces:
            src = src0 + off
            p = pm[:, off:off + w]
            if kind == "va":
                p_sw = pltpu.roll(p, HEAD_DIM, 1)
                o_ref[0, :, dst:dst + LANES] = jnp.where(low_head, p, p_sw).astype(BF16)
                o_ref[0, :, dst + LANES:dst + 2 * LANES] = jnp.where(low_head, p_sw, p).astype(BF16)
            elif kind == "v":
                o_ref[0, :, dst:dst + w] = p.astype(BF16)
            elif kind == "gate":
                o_ref[0, :, dst:dst + w] = (p * jax.nn.sigmoid(p)).astype(BF16)
            else:
                ss = jnp.dot((p * p).astype(BF16), ones_ref[0:w, 0:w], preferred_element_type=F32)
                pn = p * lax.rsqrt(ss * (1.0 / HEAD_DIM) + EPS) * cg_ref[:, src:src + w]
                for j in range(w // LANES):
                    t = pn[:, j * LANES:(j + 1) * LANES]
                    rot = jnp.where(first_half, pltpu.roll(t, LANES - HEAD_DIM // 2, 1),
                                    pltpu.roll(t, HEAD_DIM // 2, 1))
                    r = t * cos + rot * sin
                    if kind == "ka":
                        r_sw = pltpu.roll(r, HEAD_DIM, 1)
                        variants = (jnp.where(low_head, r, 0.0), jnp.where(low_head, 0.0, r_sw),
                                    jnp.where(low_head, r_sw, 0.0), jnp.where(low_head, 0.0, r))
                        for n, kvar in enumerate(variants):
                            o_ref[0, :, dst + n * LANES:dst + (n + 1) * LANES] = kvar.astype(BF16)
                    else:
                        o_ref[0, :, dst + j * LANES:dst + (j + 1) * LANES] = r.astype(BF16)


def _inproj(x, cos_t, sin_t, mod3, norm_gain, w_in_bf, colgain, ones_blk):
    bsz = x.shape[0]
    return pl.pallas_call(
        _inproj_kernel,
        grid=(bsz, SEQ // TM_IN),
        in_specs=[pl.BlockSpec((1, TM_IN, D_MODEL), lambda b, i: (b, i, 0)),
                  pl.BlockSpec((1, TM_IN, LANES), lambda b, i: (b, i, 0)),
                  pl.BlockSpec((1, TM_IN, LANES), lambda b, i: (b, i, 0)),
                  pl.BlockSpec((1, 3, D_MODEL), lambda b, i: (b, 0, 0)),
                  pl.BlockSpec((1, D_MODEL), lambda b, i: (0, 0)),
                  pl.BlockSpec((D_MODEL, IN_WIDTH), lambda b, i: (0, 0)),
                  pl.BlockSpec((1, IN_WIDTH), lambda b, i: (0, 0)),
                  pl.BlockSpec((256, 256), lambda b, i: (0, 0))],
        out_specs=pl.BlockSpec((1, TM_IN, PROJ_WIDTH), lambda b, i: (b, i, 0)),
        out_shape=jax.ShapeDtypeStruct((bsz, SEQ, PROJ_WIDTH), BF16),
        scratch_shapes=[pltpu.VMEM((TM_IN, D_MODEL), BF16)],
        compiler_params=pltpu.CompilerParams(
            dimension_semantics=("parallel", "parallel"), vmem_limit_bytes=VMEM_LIMIT),
        name="inproj",
    )(x, cos_t, sin_t, mod3, norm_gain, w_in_bf, colgain, ones_blk)


def _attn_a_kernel(sink_ref, q_ref, ke0_ref, ko0_ref, ke1_ref, ko1_ref, vd0_ref, vd1_ref, g_ref, o_ref):
    i = pl.program_id(1)
    k_refs = ((ke0_ref, ko0_ref), (ke1_ref, ko1_ref))
    v_refs = (vd0_ref, vd1_ref)
    row_minus_col = ((lax.broadcasted_iota(jnp.int32, (2 * BLOCK, SPAN), 0) & (BLOCK - 1))
                     - lax.broadcasted_iota(jnp.int32, (2 * BLOCK, SPAN), 1))
    lo_out = lax.broadcasted_iota(jnp.int32, (BLOCK, LANES), 1) < HEAD_DIM
    first_group = lax.broadcasted_iota(jnp.int32, (2 * BLOCK, 1), 0) < BLOCK
    sinks = [[jnp.where(first_group, sink_ref[4 * kv + parity], sink_ref[4 * kv + 2 + parity]) * LOG2E
              for parity in range(2)] for kv in range(2)]

    def scores(j):
        blk = i * NB_A + j
        start = pl.multiple_of(jnp.clip(blk * BLOCK - WINDOW, 0, SEQ - SPAN), BLOCK)
        valid = jnp.abs(row_minus_col + (blk * BLOCK - start)) <= WINDOW
        rows = slice(j * BLOCK, (j + 1) * BLOCK)
        out = []
        for kv in range(2):
            c0 = 2 * kv * LANES
            q2 = jnp.concatenate([q_ref[0, rows, c0:c0 + LANES], q_ref[0, rows, c0 + LANES:c0 + 2 * LANES]],
                                 axis=0)
            for parity in range(2):
                s = _nt_dot(q2, k_refs[kv][parity][0, pl.ds(start, SPAN), :])
                out.append(jnp.where(valid, s, NEG))
        return start, out

    nxt = scores(0)
    for j in range(NB_A):
        start, s_list = nxt
        if j + 1 < NB_A:
            nxt = scores(j + 1)
        rows = slice(j * BLOCK, (j + 1) * BLOCK)
        for kv in range(2):
            c0 = 2 * kv * LANES
            outs = []
            for parity in range(2):
                s = s_list[2 * kv + parity]
                sink = sinks[kv][parity]
                m = jnp.maximum(jnp.max(s, axis=-1, keepdims=True), sink)
                p = jnp.exp2(s - m)
                denom = jnp.sum(p, axis=-1, keepdims=True) + jnp.exp2(sink - m)
                o = jnp.dot(p.astype(BF16), v_refs[kv][0, pl.ds(start, SPAN), :],
                            preferred_element_type=F32)
                outs.append(o / denom)
            for grp in range(2):
                grp_rows = slice(grp * BLOCK, (grp + 1) * BLOCK)
                y = jnp.where(lo_out, outs[0][grp_rows], outs[1][grp_rows])
                cols = slice(c0 + grp * LANES, c0 + (grp + 1) * LANES)
                o_ref[0, rows, cols] = (y * g_ref[0, rows, cols].astype(F32)).astype(BF16)


def _attn_a(sink, proj):
    bsz = proj.shape[0]
    tq = NB_A * BLOCK

    def kv_spec(col):
        return pl.BlockSpec((1, SEQ, LANES), lambda b, i: (b, 0, col // LANES))

    return pl.pallas_call(
        _attn_a_kernel,
        grid=(bsz, SEQ // tq),
        in_specs=([pl.BlockSpec(memory_space=pltpu.SMEM),
                   pl.BlockSpec((1, tq, A_WIDTH), lambda b, i: (b, i, DST["qa"] // A_WIDTH))]
                  + [kv_spec(DST["ka"] + n * LANES) for n in range(KA_VARIANTS)]
                  + [kv_spec(DST["va"] + n * LANES) for n in range(VA_VARIANTS)]
                  + [pl.BlockSpec((1, tq, A_WIDTH), lambda b, i: (b, i, DST["ga"] // A_WIDTH))]),
        out_specs=pl.BlockSpec((1, tq, A_WIDTH), lambda b, i: (b, i, 0)),
        out_shape=jax.ShapeDtypeStruct((bsz, SEQ, A_WIDTH), BF16),
        compiler_params=pltpu.CompilerParams(
            dimension_semantics=("parallel", "parallel"), vmem_limit_bytes=VMEM_LIMIT),
        name="attn_a",
    )(sink, *([proj] * (2 + KA_VARIANTS + VA_VARIANTS)))


def _attn_b_kernel(lam_ref, q_ref, k_ref, v_ref, g_ref, sg_ref, o_ref):
    lq1, lk1, lq2, lk2 = (lam_ref[r:r + 1, :] for r in range(4))
    lam = (jnp.exp(jnp.sum(lq1 * lk1, axis=-1, keepdims=True))
           - jnp.exp(jnp.sum(lq2 * lk2, axis=-1, keepdims=True)) + LAMBDA_INIT)
    lo = lax.broadcasted_iota(jnp.int32, (TQ_B, LANES), 1) < HEAD_DIM
    def scores(h):
        cols = slice(h * LANES, (h + 1) * LANES)
        qp = q_ref[0, :, cols].astype(F32)
        q01 = jnp.concatenate([jnp.where(lo, qp, 0.0), jnp.where(lo, 0.0, qp)], axis=0).astype(BF16)
        return _nt_dot(q01, k_ref[0, :, cols])

    s_next = scores(0)
    for h in range(B_HEADS):
        cols = slice(h * LANES, (h + 1) * LANES)
        s = s_next
        if h + 1 < B_HEADS:
            s_next = scores(h + 1)
        m = jnp.max(s, axis=-1, keepdims=True)
        e = jnp.exp2(s - m)
        l = jnp.sum(e, axis=-1, keepdims=True)
        attn = e[:TQ_B] - e[TQ_B:] * (lam * l[:TQ_B] / l[TQ_B:])
        o = jnp.dot(attn.astype(BF16), v_ref[0, :, cols], preferred_element_type=F32)
        o = o / l[:TQ_B]
        on = o * lax.rsqrt(jnp.mean(o * o, axis=-1, keepdims=True) + EPS)
        on = on * sg_ref[...] * (1.0 - LAMBDA_INIT)
        o_ref[0, :, cols] = (on * g_ref[0, :, cols].astype(F32)).astype(BF16)


def _attn_b(lam_params, proj, subln_gain):
    bsz = proj.shape[0]
    return pl.pallas_call(
        _attn_b_kernel,
        grid=(bsz, SEQ // TQ_B),
        in_specs=[pl.BlockSpec((4, HEAD_DIM), lambda b, i: (0, 0)),
                  pl.BlockSpec((1, TQ_B, B_WIDTH), lambda b, i: (b, i, DST["qb"] // B_WIDTH)),
                  pl.BlockSpec((1, SEQ, B_WIDTH), lambda b, i: (b, 0, DST["kb"] // B_WIDTH)),
                  pl.BlockSpec((1, SEQ, B_WIDTH), lambda b, i: (b, 0, DST["vb"] // B_WIDTH)),
                  pl.BlockSpec((1, TQ_B, B_WIDTH), lambda b, i: (b, i, DST["gb"] // B_WIDTH)),
                  pl.BlockSpec((1, LANES), lambda b, i: (0, 0))],
        out_specs=pl.BlockSpec((1, TQ_B, B_WIDTH), lambda b, i: (b, i, 0)),
        out_shape=jax.ShapeDtypeStruct((bsz, SEQ, B_WIDTH), BF16),
        compiler_params=pltpu.CompilerParams(
            dimension_semantics=("parallel", "parallel"), vmem_limit_bytes=VMEM_LIMIT),
        name="attn_b",
    )(lam_params, proj, proj, proj, proj, subln_gain)


def _outproj_kernel(x_ref, ya_ref, yb_ref, w_ref, mod_ref, o_ref):
    y = (jnp.dot(ya_ref[0], w_ref[0:A_WIDTH, :], preferred_element_type=F32)
         + jnp.dot(yb_ref[0], w_ref[A_WIDTH:, :], preferred_element_type=F32))
    o_ref[0] = x_ref[0] + mod_ref[0, 2:3, :] * y


def _outproj(x, ya, yb, w_out_bf, mod3):
    bsz = x.shape[0]
    return pl.pallas_call(
        _outproj_kernel,
        grid=(bsz, SEQ // TM_OUT),
        in_specs=[pl.BlockSpec((1, TM_OUT, D_MODEL), lambda b, i: (b, i, 0)),
                  pl.BlockSpec((1, TM_OUT, A_WIDTH), lambda b, i: (b, i, 0)),
                  pl.BlockSpec((1, TM_OUT, B_WIDTH), lambda b, i: (b, i, 0)),
                  pl.BlockSpec((A_WIDTH + B_WIDTH, D_MODEL), lambda b, i: (0, 0)),
                  pl.BlockSpec((1, 3, D_MODEL), lambda b, i: (b, 0, 0))],
        out_specs=pl.BlockSpec((1, TM_OUT, D_MODEL), lambda b, i: (b, i, 0)),
        out_shape=jax.ShapeDtypeStruct((bsz, SEQ, D_MODEL), F32),
        compiler_params=pltpu.CompilerParams(
            dimension_semantics=("parallel", "parallel"), vmem_limit_bytes=VMEM_LIMIT),
        name="outproj",
    )(x, ya, yb, w_out_bf, mod3)


def _column_gains(q_norm_a, k_norm_a, q_norm_b, k_norm_b):
    g = jnp.ones((IN_WIDTH,), F32)
    g = g.at[SRC["qa"]:SRC["qa"] + 512].set(jnp.tile(q_norm_a.astype(F32), 8) * Q_PRESCALE)
    g = g.at[SRC["ka"]:SRC["ka"] + 128].set(jnp.tile(k_norm_a.astype(F32), 2))
    g = g.at[SRC["qb"]:SRC["qb"] + 512].set(jnp.tile(q_norm_b.astype(F32), 8) * Q_PRESCALE)
    g = g.at[SRC["kb"]:SRC["kb"] + 512].set(jnp.tile(k_norm_b.astype(F32), 8))
    return g.reshape(1, IN_WIDTH)


def kernel(x, c, positions, w_ada, b_ada, norm_gain, w_in, q_norm_a, k_norm_a, sink_a, q_norm_b,
           k_norm_b, lambda_q1, lambda_k1, lambda_q2, lambda_k2, subln_gain, w_out):
    assert w_ada.shape[0] == 1, "single-layer trunk"
    bsz = x.shape[0]
    inv_freq = 1.0 / (ROPE_THETA ** (jnp.arange(0, HEAD_DIM, 2, dtype=F32) / HEAD_DIM))
    ones_blk = jnp.asarray(np.kron(np.eye(4), np.ones((HEAD_DIM, HEAD_DIM))), dtype=BF16)
    colgain = _column_gains(q_norm_a[0], k_norm_a[0], q_norm_b[0], k_norm_b[0])
    lam_params = jnp.concatenate([lambda_q1, lambda_k1, lambda_q2, lambda_k2], axis=0).astype(F32)

    mod3 = _ada(c, w_ada[0], b_ada[0]).reshape(bsz, 3, D_MODEL)
    cos_t, sin_t = _rope_tables(positions, inv_freq)
    proj = _inproj(x, cos_t, sin_t, mod3, norm_gain, w_in[0].astype(BF16), colgain, ones_blk)
    ya = _attn_a(sink_a[0].astype(F32), proj)
    yb = _attn_b(lam_params, proj, subln_gain)
    return _outproj(x, ya, yb, w_out[0].astype(BF16), mod3)
```

```python
import math

import jax
import jax.numpy as jnp
import numpy as np
from jax import lax
from jax.experimental import pallas as pl
from jax.experimental.pallas import tpu as pltpu

F32 = jnp.float32
BF16 = jnp.bfloat16

D_MODEL = 1024
SEQ = 2048
HEAD_DIM = 64
LANES = 128
A_WIDTH = 512
A_KV_WIDTH = 128
B_WIDTH = 512
B_HEADS = 4
WINDOW = 128
BLOCK = 128
SPAN = BLOCK + 2 * WINDOW
ROPE_THETA = 10000.0
EPS = 1e-6
LOG2E = math.log2(math.e)
Q_PRESCALE = LOG2E / math.sqrt(HEAD_DIM)
NEG = -1e30
LAMBDA_INIT = 0.8 - 0.6 * math.exp(-0.3 * 0)

SRC = dict(qa=0, ka=512, va=640, ga=768, qb=1280, kb=1792, vb=2304, gb=2816)
IN_WIDTH = 3328
DST = dict(qa=0, ga=512, qb=1024, kb=1536, gb=2048, ka=2560, va=3072)
KA_VARIANTS = 4
VA_VARIANTS = 2
PROJ_WIDTH = 3328
VT_ROWS = 144

VMEM_LIMIT = 48 * 1024 * 1024
TM_IN = 512
NB_A = 4
TQ_B = 512
TM_OUT = 512


def _nt_dot(a, b):
    return lax.dot_general(a, b, (((1,), (1,)), ((), ())), preferred_element_type=F32)


def _ada_kernel(c_ref, w_ref, b_ref, o_ref):
    c = c_ref[...]
    sc = c * jax.nn.sigmoid(c)
    o_ref[...] = jnp.dot(sc.astype(BF16), w_ref[...].astype(BF16),
                         preferred_element_type=F32) + b_ref[...]


def _ada(c, w_ada, b_ada):
    bsz = c.shape[0]
    n = w_ada.shape[1]
    tn = 1024
    return pl.pallas_call(
        _ada_kernel,
        grid=(n // tn,),
        in_specs=[pl.BlockSpec((bsz, D_MODEL), lambda j: (0, 0)),
                  pl.BlockSpec((D_MODEL, tn), lambda j: (0, j)),
                  pl.BlockSpec((1, tn), lambda j: (0, j))],
        out_specs=pl.BlockSpec((bsz, tn), lambda j: (0, j)),
        out_shape=jax.ShapeDtypeStruct((bsz, n), F32),
        compiler_params=pltpu.CompilerParams(vmem_limit_bytes=VMEM_LIMIT),
        name="ada",
    )(c, w_ada, b_ada.reshape(1, n))


_SECTION_KIND = dict(qa="qk", ka="ka", va="va", ga="gate", qb="qk", kb="qk", vb="v", gb="gate")
_SECTION_WIDTH = dict(qa=512, ka=128, va=128, ga=512, qb=512, kb=512, vb=512, gb=512)
MM_WIDTH = 1024
EPI_WIDTH = 256


def _matmul_chunks():
    pieces = []
    for name in sorted(SRC, key=SRC.get):
        for off in range(0, _SECTION_WIDTH[name], EPI_WIDTH):
            w = min(EPI_WIDTH, _SECTION_WIDTH[name] - off)
            dst = off // LANES if name == "vb" else DST[name] + off
            pieces.append((_SECTION_KIND[name], SRC[name] + off, w, dst))
    chunks = []
    for kind, src, w, dst in pieces:
        if chunks and src + w - chunks[-1][0] <= MM_WIDTH:
            chunks[-1][2].append((kind, src - chunks[-1][0], w, dst))
            chunks[-1][1] = src + w - chunks[-1][0]
        else:
            chunks.append([src, w, [(kind, 0, w, dst)]])
    return tuple((s, w, tuple(p)) for s, w, p in chunks)


_MATMUL_CHUNKS = _matmul_chunks()


def _rope_kernel(pos_ref, invf_ref, cos_ref, sin_ref):
    ang = invf_ref[...] * pos_ref[0].astype(F32)
    cos = jnp.cos(ang)
    sin = jnp.sin(ang)
    cos4 = jnp.concatenate([cos, cos, cos, cos], axis=0)
    sin4 = jnp.concatenate([-sin, sin, -sin, sin], axis=0)
    cos_ref[0] = cos4.T
    sin_ref[0] = sin4.T


def _rope_tables(positions, inv_freq):
    bsz = positions.shape[0]
    nfreq = HEAD_DIM // 2
    return pl.pallas_call(
        _rope_kernel,
        grid=(bsz,),
        in_specs=[pl.BlockSpec((1, 1, SEQ), lambda b: (b, 0, 0)),
                  pl.BlockSpec((nfreq, 1), lambda b: (0, 0))],
        out_specs=[pl.BlockSpec((1, SEQ, LANES), lambda b: (b, 0, 0)),
                   pl.BlockSpec((1, SEQ, LANES), lambda b: (b, 0, 0))],
        out_shape=[jax.ShapeDtypeStruct((bsz, SEQ, LANES), F32)] * 2,
        compiler_params=pltpu.CompilerParams(
            dimension_semantics=("parallel",), vmem_limit_bytes=VMEM_LIMIT),
        name="rope",
    )(positions.reshape(bsz, 1, SEQ), inv_freq.reshape(nfreq, 1))


def _inproj_kernel(x_ref, cos_ref, sin_ref, mod_ref, ng_ref, w_ref, cg_ref, ones_ref,
                   o_ref, vt_ref, h_scr):
    x = x_ref[0]
    ms = jnp.mean(x * x, axis=-1, keepdims=True)
    xn = x * lax.rsqrt(ms + EPS)
    shift = mod_ref[0, 0:1, :]
    scale = mod_ref[0, 1:2, :]
    h = (xn * ng_ref[...]) * (1.0 + scale) + shift
    h_scr[...] = h.astype(BF16)

    cos = cos_ref[0]
    sin = sin_ref[0]
    lane = lax.broadcasted_iota(jnp.int32, cos.shape, 1)
    first_half = (lane & (HEAD_DIM - 1)) < HEAD_DIM // 2
    low_head = lane < HEAD_DIM

    for src0, width, pieces in _MATMUL_CHUNKS:
        pm = jnp.dot(h_scr[...], w_ref[:, src0:src0 + width], preferred_element_type=F32)
        for kind, off, w, dst in pieces:
            src = src0 + off
            p = pm[:, off:off + w]
            if kind == "va":
                p_sw = pltpu.roll(p, HEAD_DIM, 1)
                o_ref[0, :, dst:dst + LANES] = jnp.where(low_head, p, p_sw).astype(BF16)
                o_ref[0, :, dst + LANES:dst + 2 * LANES] = jnp.where(low_head, p_sw, p).astype(BF16)
            elif kind == "v":
                p_t = p.T
                for n in range(w // LANES):
                    r0 = (dst + n) * VT_ROWS
                    vt_ref[0, r0:r0 + LANES, :] = p_t[n * LANES:(n + 1) * LANES, :].astype(BF16)
                    vt_ref[0, r0 + LANES:r0 + VT_ROWS, :] = jnp.ones((VT_ROWS - LANES, p_t.shape[1]), BF16)
            elif kind == "gate":
                o_ref[0, :, dst:dst + w] = (p * jax.nn.sigmoid(p)).astype(BF16)
            else:
                ss = jnp.dot((p * p).astype(BF16), ones_ref[0:w, 0:w], preferred_element_type=F32)
                pn = p * lax.rsqrt(ss * (1.0 / HEAD_DIM) + EPS) * cg_ref[:, src:src + w]
                for j in range(w // LANES):
                    t = pn[:, j * LANES:(j + 1) * LANES]
                    rot = jnp.where(first_half, pltpu.roll(t, LANES - HEAD_DIM // 2, 1),
                                    pltpu.roll(t, HEAD_DIM // 2, 1))
                    r = t * cos + rot * sin
                    if kind == "ka":
                        r_sw = pltpu.roll(r, HEAD_DIM, 1)
                        variants = (jnp.where(low_head, r, 0.0), jnp.where(low_head, 0.0, r_sw),
                                    jnp.where(low_head, r_sw, 0.0), jnp.where(low_head, 0.0, r))
                        for n, kvar in enumerate(variants):
                            o_ref[0, :, dst + n * LANES:dst + (n + 1) * LANES] = kvar.astype(BF16)
                    else:
                        o_ref[0, :, dst + j * LANES:dst + (j + 1) * LANES] = r.astype(BF16)


def _inproj(x, cos_t, sin_t, mod3, norm_gain, w_in_bf, colgain, ones_blk):
    bsz = x.shape[0]
    return pl.pallas_call(
        _inproj_kernel,
        grid=(bsz, SEQ // TM_IN),
        in_specs=[pl.BlockSpec((1, TM_IN, D_MODEL), lambda b, i: (b, i, 0)),
                  pl.BlockSpec((1, TM_IN, LANES), lambda b, i: (b, i, 0)),
                  pl.BlockSpec((1, TM_IN, LANES), lambda b, i: (b, i, 0)),
                  pl.BlockSpec((1, 3, D_MODEL), lambda b, i: (b, 0, 0)),
                  pl.BlockSpec((1, D_MODEL), lambda b, i: (0, 0)),
                  pl.BlockSpec((D_MODEL, IN_WIDTH), lambda b, i: (0, 0)),
                  pl.BlockSpec((1, IN_WIDTH), lambda b, i: (0, 0)),
                  pl.BlockSpec((256, 256), lambda b, i: (0, 0))],
        out_specs=[pl.BlockSpec((1, TM_IN, PROJ_WIDTH), lambda b, i: (b, i, 0)),
                   pl.BlockSpec((1, B_HEADS * VT_ROWS, TM_IN), lambda b, i: (b, 0, i))],
        out_shape=[jax.ShapeDtypeStruct((bsz, SEQ, PROJ_WIDTH), BF16),
                   jax.ShapeDtypeStruct((bsz, B_HEADS * VT_ROWS, SEQ), BF16)],
        scratch_shapes=[pltpu.VMEM((TM_IN, D_MODEL), BF16)],
        compiler_params=pltpu.CompilerParams(
            dimension_semantics=("parallel", "parallel"), vmem_limit_bytes=VMEM_LIMIT),
        name="inproj",
    )(x, cos_t, sin_t, mod3, norm_gain, w_in_bf, colgain, ones_blk)


def _attn_a_kernel(sink_ref, q_ref, ke0_ref, ko0_ref, ke1_ref, ko1_ref, vd0_ref, vd1_ref, g_ref, o_ref):
    i = pl.program_id(1)
    k_refs = ((ke0_ref, ko0_ref), (ke1_ref, ko1_ref))
    v_refs = (vd0_ref, vd1_ref)
    row_minus_col = ((lax.broadcasted_iota(jnp.int32, (2 * BLOCK, SPAN), 0) & (BLOCK - 1))
                     - lax.broadcasted_iota(jnp.int32, (2 * BLOCK, SPAN), 1))
    lo_out = lax.broadcasted_iota(jnp.int32, (BLOCK, LANES), 1) < HEAD_DIM
    first_group = lax.broadcasted_iota(jnp.int32, (2 * BLOCK, 1), 0) < BLOCK
    sinks = [[jnp.where(first_group, sink_ref[4 * kv + parity], sink_ref[4 * kv + 2 + parity]) * LOG2E
              for parity in range(2)] for kv in range(2)]

    def scores(j):
        blk = i * NB_A + j
        start = pl.multiple_of(jnp.clip(blk * BLOCK - WINDOW, 0, SEQ - SPAN), BLOCK)
        valid = jnp.abs(row_minus_col + (blk * BLOCK - start)) <= WINDOW
        rows = slice(j * BLOCK, (j + 1) * BLOCK)
        out = []
        for kv in range(2):
            c0 = 2 * kv * LANES
            q2 = jnp.concatenate([q_ref[0, rows, c0:c0 + LANES], q_ref[0, rows, c0 + LANES:c0 + 2 * LANES]],
                                 axis=0)
            for parity in range(2):
                s = _nt_dot(q2, k_refs[kv][parity][0, pl.ds(start, SPAN), :])
                out.append(jnp.where(valid, s, NEG))
        return start, out

    nxt = scores(0)
    for j in range(NB_A):
        start, s_list = nxt
        if j + 1 < NB_A:
            nxt = scores(j + 1)
        rows = slice(j * BLOCK, (j + 1) * BLOCK)
        for kv in range(2):
            c0 = 2 * kv * LANES
            outs = []
            for parity in range(2):
                s = s_list[2 * kv + parity]
                sink = sinks[kv][parity]
                m = jnp.maximum(jnp.max(s, axis=-1, keepdims=True), sink)
                p = jnp.exp2(s - m)
                denom = jnp.sum(p, axis=-1, keepdims=True) + jnp.exp2(sink - m)
                o = jnp.dot(p.astype(BF16), v_refs[kv][0, pl.ds(start, SPAN), :],
                            preferred_element_type=F32)
                outs.append(o / denom)
            for grp in range(2):
                grp_rows = slice(grp * BLOCK, (grp + 1) * BLOCK)
                y = jnp.where(lo_out, outs[0][grp_rows], outs[1][grp_rows])
                cols = slice(c0 + grp * LANES, c0 + (grp + 1) * LANES)
                o_ref[0, rows, cols] = (y * g_ref[0, rows, cols].astype(F32)).astype(BF16)


def _attn_a(sink, proj):
    bsz = proj.shape[0]
    tq = NB_A * BLOCK

    def kv_spec(col):
        return pl.BlockSpec((1, SEQ, LANES), lambda b, i: (b, 0, col // LANES))

    return pl.pallas_call(
        _attn_a_kernel,
        grid=(bsz, SEQ // tq),
        in_specs=([pl.BlockSpec(memory_space=pltpu.SMEM),
                   pl.BlockSpec((1, tq, A_WIDTH), lambda b, i: (b, i, DST["qa"] // A_WIDTH))]
                  + [kv_spec(DST["ka"] + n * LANES) for n in range(KA_VARIANTS)]
                  + [kv_spec(DST["va"] + n * LANES) for n in range(VA_VARIANTS)]
                  + [pl.BlockSpec((1, tq, A_WIDTH), lambda b, i: (b, i, DST["ga"] // A_WIDTH))]),
        out_specs=pl.BlockSpec((1, tq, A_WIDTH), lambda b, i: (b, i, 0)),
        out_shape=jax.ShapeDtypeStruct((bsz, SEQ, A_WIDTH), BF16),
        compiler_params=pltpu.CompilerParams(
            dimension_semantics=("parallel", "parallel"), vmem_limit_bytes=VMEM_LIMIT),
        name="attn_a",
    )(sink, *([proj] * (2 + KA_VARIANTS + VA_VARIANTS)))


def _attn_b_kernel(lam_ref, q_ref, k_ref, vt_ref, g_ref, sg_ref, o_ref):
    lq1, lk1, lq2, lk2 = (lam_ref[r:r + 1, :] for r in range(4))
    lam = (jnp.exp(jnp.sum(lq1 * lk1, axis=-1, keepdims=True))
           - jnp.exp(jnp.sum(lq2 * lk2, axis=-1, keepdims=True)) + LAMBDA_INIT)
    lo = lax.broadcasted_iota(jnp.int32, (TQ_B, LANES), 1) < HEAD_DIM

    def scores_t(h):
        cols = slice(h * LANES, (h + 1) * LANES)
        qp = q_ref[0, :, cols].astype(F32)
        q01 = jnp.concatenate([jnp.where(lo, qp, 0.0), jnp.where(lo, 0.0, qp)], axis=0).astype(BF16)
        return _nt_dot(k_ref[0, :, cols], q01)

    s_next = scores_t(0)
    for h in range(B_HEADS):
        cols = slice(h * LANES, (h + 1) * LANES)
        s_t = s_next
        if h + 1 < B_HEADS:
            s_next = scores_t(h + 1)
        m = jnp.max(s_t, axis=0, keepdims=True)
        e_t = jnp.exp2(s_t - m).astype(BF16)
        ol = jnp.dot(vt_ref[0, h * VT_ROWS:(h + 1) * VT_ROWS, :], e_t,
                     preferred_element_type=F32)
        o0 = ol[:LANES, :TQ_B] / ol[LANES:LANES + 1, :TQ_B]
        o1 = ol[:LANES, TQ_B:] / ol[LANES:LANES + 1, TQ_B:]
        o = (o0 - lam * o1).T
        on = o * lax.rsqrt(jnp.mean(o * o, axis=-1, keepdims=True) + EPS)
        on = on * sg_ref[...] * (1.0 - LAMBDA_INIT)
        o_ref[0, :, cols] = (on * g_ref[0, :, cols].astype(F32)).astype(BF16)


def _attn_b(lam_params, proj, vt, subln_gain):
    bsz = proj.shape[0]
    return pl.pallas_call(
        _attn_b_kernel,
        grid=(bsz, SEQ // TQ_B),
        in_specs=[pl.BlockSpec((4, HEAD_DIM), lambda b, i: (0, 0)),
                  pl.BlockSpec((1, TQ_B, B_WIDTH), lambda b, i: (b, i, DST["qb"] // B_WIDTH)),
                  pl.BlockSpec((1, SEQ, B_WIDTH), lambda b, i: (b, 0, DST["kb"] // B_WIDTH)),
                  pl.BlockSpec((1, B_HEADS * VT_ROWS, SEQ), lambda b, i: (b, 0, 0)),
                  pl.BlockSpec((1, TQ_B, B_WIDTH), lambda b, i: (b, i, DST["gb"] // B_WIDTH)),
                  pl.BlockSpec((1, LANES), lambda b, i: (0, 0))],
        out_specs=pl.BlockSpec((1, TQ_B, B_WIDTH), lambda b, i: (b, i, 0)),
        out_shape=jax.ShapeDtypeStruct((bsz, SEQ, B_WIDTH), BF16),
        compiler_params=pltpu.CompilerParams(
            dimension_semantics=("parallel", "parallel"), vmem_limit_bytes=VMEM_LIMIT),
        name="attn_b",
    )(lam_params, proj, proj, vt, proj, subln_gain)


def _outproj_kernel(x_ref, ya_ref, yb_ref, w_ref, mod_ref, o_ref):
    y = (jnp.dot(ya_ref[0], w_ref[0:A_WIDTH, :], preferred_element_type=F32)
         + jnp.dot(yb_ref[0], w_ref[A_WIDTH:, :], preferred_element_type=F32))
    o_ref[0] = x_ref[0] + mod_ref[0, 2:3, :] * y


def _outproj(x, ya, yb, w_out_bf, mod3):
    bsz = x.shape[0]
    return pl.pallas_call(
        _outproj_kernel,
        grid=(bsz, SEQ // TM_OUT),
        in_specs=[pl.BlockSpec((1, TM_OUT, D_MODEL), lambda b, i: (b, i, 0)),
                  pl.BlockSpec((1, TM_OUT, A_WIDTH), lambda b, i: (b, i, 0)),
                  pl.BlockSpec((1, TM_OUT, B_WIDTH), lambda b, i: (b, i, 0)),
                  pl.BlockSpec((A_WIDTH + B_WIDTH, D_MODEL), lambda b, i: (0, 0)),
                  pl.BlockSpec((1, 3, D_MODEL), lambda b, i: (b, 0, 0))],
        out_specs=pl.BlockSpec((1, TM_OUT, D_MODEL), lambda b, i: (b, i, 0)),
        out_shape=jax.ShapeDtypeStruct((bsz, SEQ, D_MODEL), F32),
        compiler_params=pltpu.CompilerParams(
            dimension_semantics=("parallel", "parallel"), vmem_limit_bytes=VMEM_LIMIT),
        name="outproj",
    )(x, ya, yb, w_out_bf, mod3)


def _column_gains(q_norm_a, k_norm_a, q_norm_b, k_norm_b):
    g = jnp.ones((IN_WIDTH,), F32)
    g = g.at[SRC["qa"]:SRC["qa"] + 512].set(jnp.tile(q_norm_a.astype(F32), 8) * Q_PRESCALE)
    g = g.at[SRC["ka"]:SRC["ka"] + 128].set(jnp.tile(k_norm_a.astype(F32), 2))
    g = g.at[SRC["qb"]:SRC["qb"] + 512].set(jnp.tile(q_norm_b.astype(F32), 8) * Q_PRESCALE)
    g = g.at[SRC["kb"]:SRC["kb"] + 512].set(jnp.tile(k_norm_b.astype(F32), 8))
    return g.reshape(1, IN_WIDTH)


def kernel(x, c, positions, w_ada, b_ada, norm_gain, w_in, q_norm_a, k_norm_a, sink_a, q_norm_b,
           k_norm_b, lambda_q1, lambda_k1, lambda_q2, lambda_k2, subln_gain, w_out):
    assert w_ada.shape[0] == 1, "single-layer trunk"
    bsz = x.shape[0]
    inv_freq = 1.0 / (ROPE_THETA ** (jnp.arange(0, HEAD_DIM, 2, dtype=F32) / HEAD_DIM))
    ones_blk = jnp.asarray(np.kron(np.eye(4), np.ones((HEAD_DIM, HEAD_DIM))), dtype=BF16)
    colgain = _column_gains(q_norm_a[0], k_norm_a[0], q_norm_b[0], k_norm_b[0])
    lam_params = jnp.concatenate([lambda_q1, lambda_k1, lambda_q2, lambda_k2], axis=0).astype(F32)

    mod3 = _ada(c, w_ada[0], b_ada[0]).reshape(bsz, 3, D_MODEL)
    cos_t, sin_t = _rope_tables(positions, inv_freq)
    proj, vt = _inproj(x, cos_t, sin_t, mod3, norm_gain, w_in[0].astype(BF16), colgain, ones_blk)
    ya = _attn_a(sink_a[0].astype(F32), proj)
    yb = _attn_b(lam_params, proj, vt, subln_gain)
    return _outproj(x, ya, yb, w_out[0].astype(BF16), mod3)
```

```python
import math

import jax
import jax.numpy as jnp
import numpy as np
from jax import lax
from jax.experimental import pallas as pl
from jax.experimental.pallas import tpu as pltpu

F32 = jnp.float32
BF16 = jnp.bfloat16

D_MODEL = 1024
SEQ = 2048
HEAD_DIM = 64
LANES = 128
A_WIDTH = 512
A_KV_WIDTH = 128
B_WIDTH = 512
B_HEADS = 4
WINDOW = 128
BLOCK = 128
SPAN = BLOCK + 2 * WINDOW
ROPE_THETA = 10000.0
EPS = 1e-6
LOG2E = math.log2(math.e)
Q_PRESCALE = LOG2E / math.sqrt(HEAD_DIM)
NEG = -1e30
LAMBDA_INIT = 0.8 - 0.6 * math.exp(-0.3 * 0)

SRC = dict(qa=0, ka=512, va=640, ga=768, qb=1280, kb=1792, vb=2304, gb=2816)
IN_WIDTH = 3328
DST = dict(qa=0, ga=512, qb=1024, kb=1536, gb=2048, ka=2560, va=3072)
KA_VARIANTS = 4
VA_VARIANTS = 2
PROJ_WIDTH = 3328
VT_ROWS = 144

VMEM_LIMIT = 48 * 1024 * 1024
TM_IN = 512
NB_A = 4
TQ_B = 512
QK_AHEAD_B = 1
OUT_BLOCKS_A = 2


def _nt_dot(a, b):
    return lax.dot_general(a, b, (((1,), (1,)), ((), ())), preferred_element_type=F32)


def _ada_kernel(c_ref, w_ref, b_ref, o_ref):
    c = c_ref[...]
    sc = c * jax.nn.sigmoid(c)
    o_ref[...] = jnp.dot(sc.astype(BF16), w_ref[...].astype(BF16),
                         preferred_element_type=F32) + b_ref[...]


def _ada(c, w_ada, b_ada):
    bsz = c.shape[0]
    n = w_ada.shape[1]
    tn = 1024
    return pl.pallas_call(
        _ada_kernel,
        grid=(n // tn,),
        in_specs=[pl.BlockSpec((bsz, D_MODEL), lambda j: (0, 0)),
                  pl.BlockSpec((D_MODEL, tn), lambda j: (0, j)),
                  pl.BlockSpec((1, tn), lambda j: (0, j))],
        out_specs=pl.BlockSpec((bsz, tn), lambda j: (0, j)),
        out_shape=jax.ShapeDtypeStruct((bsz, n), F32),
        compiler_params=pltpu.CompilerParams(vmem_limit_bytes=VMEM_LIMIT),
        name="ada",
    )(c, w_ada, b_ada.reshape(1, n))


_SECTION_KIND = dict(qa="qk", ka="ka", va="va", ga="gate", qb="qk", kb="qk", vb="v", gb="gate")
_SECTION_WIDTH = dict(qa=512, ka=128, va=128, ga=512, qb=512, kb=512, vb=512, gb=512)
MM_WIDTH = 1024
EPI_WIDTH = 256


def _matmul_chunks():
    pieces = []
    for name in sorted(SRC, key=SRC.get):
        for off in range(0, _SECTION_WIDTH[name], EPI_WIDTH):
            w = min(EPI_WIDTH, _SECTION_WIDTH[name] - off)
            dst = off // LANES if name == "vb" else DST[name] + off
            pieces.append((_SECTION_KIND[name], SRC[name] + off, w, dst))
    chunks = []
    for kind, src, w, dst in pieces:
        if chunks and src + w - chunks[-1][0] <= MM_WIDTH:
            chunks[-1][2].append((kind, src - chunks[-1][0], w, dst))
            chunks[-1][1] = src + w - chunks[-1][0]
        else:
            chunks.append([src, w, [(kind, 0, w, dst)]])
    return tuple((s, w, tuple(p)) for s, w, p in chunks)


_MATMUL_CHUNKS = _matmul_chunks()


def _rope_kernel(pos_ref, invf_ref, cos_ref, sin_ref):
    ang = invf_ref[...] * pos_ref[0].astype(F32)
    cos = jnp.cos(ang)
    sin = jnp.sin(ang)
    cos4 = jnp.concatenate([cos, cos, cos, cos], axis=0)
    sin4 = jnp.concatenate([-sin, sin, -sin, sin], axis=0)
    cos_ref[0] = cos4.T
    sin_ref[0] = sin4.T


def _rope_tables(positions, inv_freq):
    bsz = positions.shape[0]
    nfreq = HEAD_DIM // 2
    return pl.pallas_call(
        _rope_kernel,
        grid=(bsz,),
        in_specs=[pl.BlockSpec((1, 1, SEQ), lambda b: (b, 0, 0)),
                  pl.BlockSpec((nfreq, 1), lambda b: (0, 0))],
        out_specs=[pl.BlockSpec((1, SEQ, LANES), lambda b: (b, 0, 0)),
                   pl.BlockSpec((1, SEQ, LANES), lambda b: (b, 0, 0))],
        out_shape=[jax.ShapeDtypeStruct((bsz, SEQ, LANES), F32)] * 2,
        compiler_params=pltpu.CompilerParams(
            dimension_semantics=("parallel",), vmem_limit_bytes=VMEM_LIMIT),
        name="rope",
    )(positions.reshape(bsz, 1, SEQ), inv_freq.reshape(nfreq, 1))


def _inproj_kernel(x_ref, cos_ref, sin_ref, mod_ref, ng_ref, w_ref, cg_ref, ones_ref,
                   o_ref, vt_ref, h_scr):
    x = x_ref[0]
    ms = jnp.mean(x * x, axis=-1, keepdims=True)
    xn = x * lax.rsqrt(ms + EPS)
    shift = mod_ref[0, 0:1, :]
    scale = mod_ref[0, 1:2, :]
    h = (xn * ng_ref[...]) * (1.0 + scale) + shift
    h_scr[...] = h.astype(BF16)

    cos = cos_ref[0]
    sin = sin_ref[0]
    lane = lax.broadcasted_iota(jnp.int32, cos.shape, 1)
    first_half = (lane & (HEAD_DIM - 1)) < HEAD_DIM // 2
    low_head = lane < HEAD_DIM

    for src0, width, pieces in _MATMUL_CHUNKS:
        pm = jnp.dot(h_scr[...], w_ref[:, src0:src0 + width], preferred_element_type=F32)
        for kind, off, w, dst in pieces:
            src = src0 + off
            p = pm[:, off:off + w]
            if kind == "va":
                p_sw = pltpu.roll(p, HEAD_DIM, 1)
                o_ref[0, :, dst:dst + LANES] = jnp.where(low_head, p, p_sw).astype(BF16)
                o_ref[0, :, dst + LANES:dst + 2 * LANES] = jnp.where(low_head, p_sw, p).astype(BF16)
            elif kind == "v":
                p_t = p.T
                for n in range(w // LANES):
                    r0 = (dst + n) * VT_ROWS
                    vt_ref[0, r0:r0 + LANES, :] = p_t[n * LANES:(n + 1) * LANES, :].astype(BF16)
                    vt_ref[0, r0 + LANES:r0 + VT_ROWS, :] = jnp.ones((VT_ROWS - LANES, p_t.shape[1]), BF16)
            elif kind == "gate":
                o_ref[0, :, dst:dst + w] = (p * jax.nn.sigmoid(p)).astype(BF16)
            else:
                ss = jnp.dot((p * p).astype(BF16), ones_ref[0:w, 0:w], preferred_element_type=F32)
                pn = p * lax.rsqrt(ss * (1.0 / HEAD_DIM) + EPS) * cg_ref[:, src:src + w]
                for j in range(w // LANES):
                    t = pn[:, j * LANES:(j + 1) * LANES]
                    rot = jnp.where(first_half, pltpu.roll(t, LANES - HEAD_DIM // 2, 1),
                                    pltpu.roll(t, HEAD_DIM // 2, 1))
                    r = t * cos + rot * sin
                    if kind == "ka":
                        r_sw = pltpu.roll(r, HEAD_DIM, 1)
                        variants = (jnp.where(low_head, r, 0.0), jnp.where(low_head, 0.0, r_sw),
                                    jnp.where(low_head, r_sw, 0.0), jnp.where(low_head, 0.0, r))
                        for n, kvar in enumerate(variants):
                            o_ref[0, :, dst + n * LANES:dst + (n + 1) * LANES] = kvar.astype(BF16)
                    else:
                        o_ref[0, :, dst + j * LANES:dst + (j + 1) * LANES] = r.astype(BF16)


def _inproj(x, cos_t, sin_t, mod3, norm_gain, w_in_bf, colgain, ones_blk):
    bsz = x.shape[0]
    return pl.pallas_call(
        _inproj_kernel,
        grid=(bsz, SEQ // TM_IN),
        in_specs=[pl.BlockSpec((1, TM_IN, D_MODEL), lambda b, i: (b, i, 0)),
                  pl.BlockSpec((1, TM_IN, LANES), lambda b, i: (b, i, 0)),
                  pl.BlockSpec((1, TM_IN, LANES), lambda b, i: (b, i, 0)),
                  pl.BlockSpec((1, 3, D_MODEL), lambda b, i: (b, 0, 0)),
                  pl.BlockSpec((1, D_MODEL), lambda b, i: (0, 0)),
                  pl.BlockSpec((D_MODEL, IN_WIDTH), lambda b, i: (0, 0)),
                  pl.BlockSpec((1, IN_WIDTH), lambda b, i: (0, 0)),
                  pl.BlockSpec((256, 256), lambda b, i: (0, 0))],
        out_specs=[pl.BlockSpec((1, TM_IN, PROJ_WIDTH), lambda b, i: (b, i, 0)),
                   pl.BlockSpec((1, B_HEADS * VT_ROWS, TM_IN), lambda b, i: (b, 0, i))],
        out_shape=[jax.ShapeDtypeStruct((bsz, SEQ, PROJ_WIDTH), BF16),
                   jax.ShapeDtypeStruct((bsz, B_HEADS * VT_ROWS, SEQ), BF16)],
        scratch_shapes=[pltpu.VMEM((TM_IN, D_MODEL), BF16)],
        compiler_params=pltpu.CompilerParams(
            dimension_semantics=("parallel", "parallel"), vmem_limit_bytes=VMEM_LIMIT),
        name="inproj",
    )(x, cos_t, sin_t, mod3, norm_gain, w_in_bf, colgain, ones_blk)


def _attn_a_out_kernel(sink_ref, q_ref, ke0_ref, ko0_ref, ke1_ref, ko1_ref, vd0_ref, vd1_ref, g_ref,
                       x_ref, yb_ref, wo_ref, mod_ref, o_ref, ya_scr):
    i = pl.program_id(1)
    k_refs = ((ke0_ref, ko0_ref), (ke1_ref, ko1_ref))
    v_refs = (vd0_ref, vd1_ref)
    row_minus_col = ((lax.broadcasted_iota(jnp.int32, (2 * BLOCK, SPAN), 0) & (BLOCK - 1))
                     - lax.broadcasted_iota(jnp.int32, (2 * BLOCK, SPAN), 1))
    lo_out = lax.broadcasted_iota(jnp.int32, (BLOCK, LANES), 1) < HEAD_DIM
    first_group = lax.broadcasted_iota(jnp.int32, (2 * BLOCK, 1), 0) < BLOCK
    sinks = [[jnp.where(first_group, sink_ref[4 * kv + parity], sink_ref[4 * kv + 2 + parity]) * LOG2E
              for parity in range(2)] for kv in range(2)]

    def scores(j):
        blk = i * NB_A + j
        start = pl.multiple_of(jnp.clip(blk * BLOCK - WINDOW, 0, SEQ - SPAN), BLOCK)
        valid = jnp.abs(row_minus_col + (blk * BLOCK - start)) <= WINDOW
        rows = slice(j * BLOCK, (j + 1) * BLOCK)
        out = []
        for kv in range(2):
            c0 = 2 * kv * LANES
            q2 = jnp.concatenate([q_ref[0, rows, c0:c0 + LANES], q_ref[0, rows, c0 + LANES:c0 + 2 * LANES]],
                                 axis=0)
            for parity in range(2):
                s = _nt_dot(q2, k_refs[kv][parity][0, pl.ds(start, SPAN), :])
                out.append(jnp.where(valid, s, NEG))
        return start, out

    nxt = scores(0)
    for j in range(NB_A):
        start, s_list = nxt
        if j + 1 < NB_A:
            nxt = scores(j + 1)
        rows = slice(j * BLOCK, (j + 1) * BLOCK)
        for kv in range(2):
            c0 = 2 * kv * LANES
            outs = []
            for parity in range(2):
                s = s_list[2 * kv + parity]
                sink = sinks[kv][parity]
                m = jnp.maximum(jnp.max(s, axis=-1, keepdims=True), sink)
                p = jnp.exp2(s - m)
                denom = jnp.sum(p, axis=-1, keepdims=True) + jnp.exp2(sink - m)
                o = jnp.dot(p.astype(BF16), v_refs[kv][0, pl.ds(start, SPAN), :],
                            preferred_element_type=F32)
                outs.append(o / denom)
            for grp in range(2):
                grp_rows = slice(grp * BLOCK, (grp + 1) * BLOCK)
                y = jnp.where(lo_out, outs[0][grp_rows], outs[1][grp_rows])
                cols = slice(c0 + grp * LANES, c0 + (grp + 1) * LANES)
                ya_scr[rows, cols] = (y * g_ref[0, rows, cols].astype(F32)).astype(BF16)
        if (j + 1) % OUT_BLOCKS_A == 0:
            out_rows = slice((j + 1 - OUT_BLOCKS_A) * BLOCK, (j + 1) * BLOCK)
            y_out = (jnp.dot(ya_scr[out_rows, :], wo_ref[0:A_WIDTH, :], preferred_element_type=F32)
                     + jnp.dot(yb_ref[0, out_rows, :], wo_ref[A_WIDTH:, :], preferred_element_type=F32))
            o_ref[0, out_rows, :] = x_ref[0, out_rows, :] + mod_ref[0, 2:3, :] * y_out


def _attn_a_out(sink, proj, x, yb, w_out_bf, mod3):
    bsz = proj.shape[0]
    tq = NB_A * BLOCK

    def kv_spec(col):
        return pl.BlockSpec((1, SEQ, LANES), lambda b, i: (b, 0, col // LANES))

    return pl.pallas_call(
        _attn_a_out_kernel,
        grid=(bsz, SEQ // tq),
        in_specs=([pl.BlockSpec(memory_space=pltpu.SMEM),
                   pl.BlockSpec((1, tq, A_WIDTH), lambda b, i: (b, i, DST["qa"] // A_WIDTH))]
                  + [kv_spec(DST["ka"] + n * LANES) for n in range(KA_VARIANTS)]
                  + [kv_spec(DST["va"] + n * LANES) for n in range(VA_VARIANTS)]
                  + [pl.BlockSpec((1, tq, A_WIDTH), lambda b, i: (b, i, DST["ga"] // A_WIDTH)),
                     pl.BlockSpec((1, tq, D_MODEL), lambda b, i: (b, i, 0)),
                     pl.BlockSpec((1, tq, B_WIDTH), lambda b, i: (b, i, 0)),
                     pl.BlockSpec((A_WIDTH + B_WIDTH, D_MODEL), lambda b, i: (0, 0)),
                     pl.BlockSpec((1, 3, D_MODEL), lambda b, i: (b, 0, 0))]),
        out_specs=pl.BlockSpec((1, tq, D_MODEL), lambda b, i: (b, i, 0)),
        out_shape=jax.ShapeDtypeStruct((bsz, SEQ, D_MODEL), F32),
        scratch_shapes=[pltpu.VMEM((tq, A_WIDTH), BF16)],
        compiler_params=pltpu.CompilerParams(
            dimension_semantics=("parallel", "parallel"), vmem_limit_bytes=VMEM_LIMIT),
        name="attn_a_out",
    )(sink, *([proj] * (2 + KA_VARIANTS + VA_VARIANTS)), x, yb, w_out_bf, mod3)


def _attn_b_kernel(lam_ref, q_ref, k_ref, vt_ref, g_ref, sg_ref, o_ref):
    lq1, lk1, lq2, lk2 = (lam_ref[r:r + 1, :] for r in range(4))
    lam = (jnp.exp(jnp.sum(lq1 * lk1, axis=-1, keepdims=True))
           - jnp.exp(jnp.sum(lq2 * lk2, axis=-1, keepdims=True)) + LAMBDA_INIT)
    lo = lax.broadcasted_iota(jnp.int32, (TQ_B, LANES), 1) < HEAD_DIM

    def scores_t(h):
        cols = slice(h * LANES, (h + 1) * LANES)
        qp = q_ref[0, :, cols].astype(F32)
        q01 = jnp.concatenate([jnp.where(lo, qp, 0.0), jnp.where(lo, 0.0, qp)], axis=0).astype(BF16)
        return _nt_dot(k_ref[0, :, cols], q01)

    ahead = [scores_t(h) for h in range(QK_AHEAD_B)]
    for h in range(B_HEADS):
        cols = slice(h * LANES, (h + 1) * LANES)
        s_t = ahead.pop(0)
        if h + QK_AHEAD_B < B_HEADS:
            ahead.append(scores_t(h + QK_AHEAD_B))
        m = jnp.max(s_t, axis=0, keepdims=True)
        e_t = jnp.exp2(s_t - m).astype(BF16)
        ol = jnp.dot(vt_ref[0, h * VT_ROWS:(h + 1) * VT_ROWS, :], e_t,
                     preferred_element_type=F32)
        o0 = ol[:LANES, :TQ_B] / ol[LANES:LANES + 1, :TQ_B]
        o1 = ol[:LANES, TQ_B:] / ol[LANES:LANES + 1, TQ_B:]
        o = (o0 - lam * o1).T
        on = o * lax.rsqrt(jnp.mean(o * o, axis=-1, keepdims=True) + EPS)
        on = on * sg_ref[...] * (1.0 - LAMBDA_INIT)
        o_ref[0, :, cols] = (on * g_ref[0, :, cols].astype(F32)).astype(BF16)


def _attn_b(lam_params, proj, vt, subln_gain):
    bsz = proj.shape[0]
    return pl.pallas_call(
        _attn_b_kernel,
        grid=(bsz, SEQ // TQ_B),
        in_specs=[pl.BlockSpec((4, HEAD_DIM), lambda b, i: (0, 0)),
                  pl.BlockSpec((1, TQ_B, B_WIDTH), lambda b, i: (b, i, DST["qb"] // B_WIDTH)),
                  pl.BlockSpec((1, SEQ, B_WIDTH), lambda b, i: (b, 0, DST["kb"] // B_WIDTH)),
                  pl.BlockSpec((1, B_HEADS * VT_ROWS, SEQ), lambda b, i: (b, 0, 0)),
                  pl.BlockSpec((1, TQ_B, B_WIDTH), lambda b, i: (b, i, DST["gb"] // B_WIDTH)),
                  pl.BlockSpec((1, LANES), lambda b, i: (0, 0))],
        out_specs=pl.BlockSpec((1, TQ_B, B_WIDTH), lambda b, i: (b, i, 0)),
        out_shape=jax.ShapeDtypeStruct((bsz, SEQ, B_WIDTH), BF16),
        compiler_params=pltpu.CompilerParams(
            dimension_semantics=("parallel", "parallel"), vmem_limit_bytes=VMEM_LIMIT),
        name="attn_b",
    )(lam_params, proj, proj, vt, proj, subln_gain)


def _column_gains(q_norm_a, k_norm_a, q_norm_b, k_norm_b):
    g = jnp.ones((IN_WIDTH,), F32)
    g = g.at[SRC["qa"]:SRC["qa"] + 512].set(jnp.tile(q_norm_a.astype(F32), 8) * Q_PRESCALE)
    g = g.at[SRC["ka"]:SRC["ka"] + 128].set(jnp.tile(k_norm_a.astype(F32), 2))
    g = g.at[SRC["qb"]:SRC["qb"] + 512].set(jnp.tile(q_norm_b.astype(F32), 8) * Q_PRESCALE)
    g = g.at[SRC["kb"]:SRC["kb"] + 512].set(jnp.tile(k_norm_b.astype(F32), 8))
    return g.reshape(1, IN_WIDTH)


def kernel(x, c, positions, w_ada, b_ada, norm_gain, w_in, q_norm_a, k_norm_a, sink_a, q_norm_b,
           k_norm_b, lambda_q1, lambda_k1, lambda_q2, lambda_k2, subln_gain, w_out):
    assert w_ada.shape[0] == 1, "single-layer trunk"
    bsz = x.shape[0]
    inv_freq = 1.0 / (ROPE_THETA ** (jnp.arange(0, HEAD_DIM, 2, dtype=F32) / HEAD_DIM))
    ones_blk = jnp.asarray(np.kron(np.eye(4), np.ones((HEAD_DIM, HEAD_DIM))), dtype=BF16)
    colgain = _column_gains(q_norm_a[0], k_norm_a[0], q_norm_b[0], k_norm_b[0])
    lam_params = jnp.concatenate([lambda_q1, lambda_k1, lambda_q2, lambda_k2], axis=0).astype(F32)

    mod3 = _ada(c, w_ada[0], b_ada[0]).reshape(bsz, 3, D_MODEL)
    cos_t, sin_t = _rope_tables(positions, inv_freq)
    proj, vt = _inproj(x, cos_t, sin_t, mod3, norm_gain, w_in[0].astype(BF16), colgain, ones_blk)
    yb = _attn_b(lam_params, proj, vt, subln_gain)
    return _attn_a_out(sink_a[0].astype(F32), proj, x, yb, w_out[0].astype(BF16), mod3)
```

```python
import math

import jax
import jax.numpy as jnp
import numpy as np
from jax import lax
from jax.experimental import pallas as pl
from jax.experimental.pallas import tpu as pltpu

F32 = jnp.float32
BF16 = jnp.bfloat16

D_MODEL = 1024
SEQ = 2048
HEAD_DIM = 64
LANES = 128
A_WIDTH = 512
A_KV_WIDTH = 128
B_WIDTH = 512
B_HEADS = 4
WINDOW = 128
BLOCK = 128
SPAN = BLOCK + 2 * WINDOW
ROPE_THETA = 10000.0
EPS = 1e-6
LOG2E = math.log2(math.e)
Q_PRESCALE = LOG2E / math.sqrt(HEAD_DIM)
NEG = -1e30
LAMBDA_INIT = 0.8 - 0.6 * math.exp(-0.3 * 0)

SRC = dict(qa=0, ka=512, va=640, ga=768, qb=1280, kb=1792, vb=2304, gb=2816)
IN_WIDTH = 3328
DST = dict(qa=0, ga=512, qb=1024, kb=1536, gb=2048, ka=2560, va=3072)
KA_VARIANTS = 4
VA_VARIANTS = 2
PROJ_WIDTH = 3328
VT_ROWS = 144

VMEM_LIMIT = 48 * 1024 * 1024
TM_IN = 512
NB_A = 8
TQ_B = 512
QT_B = 4
QK_AHEAD_B = 1
OUT_BLOCKS_A = 4


def _nt_dot(a, b):
    return lax.dot_general(a, b, (((1,), (1,)), ((), ())), preferred_element_type=F32)


def _ada_kernel(c_ref, w_ref, b_ref, o_ref):
    c = c_ref[...]
    sc = c * jax.nn.sigmoid(c)
    o_ref[...] = jnp.dot(sc.astype(BF16), w_ref[...].astype(BF16),
                         preferred_element_type=F32) + b_ref[...]


def _ada(c, w_ada, b_ada):
    bsz = c.shape[0]
    n = w_ada.shape[1]
    tn = 1024
    return pl.pallas_call(
        _ada_kernel,
        grid=(n // tn,),
        in_specs=[pl.BlockSpec((bsz, D_MODEL), lambda j: (0, 0)),
                  pl.BlockSpec((D_MODEL, tn), lambda j: (0, j)),
                  pl.BlockSpec((1, tn), lambda j: (0, j))],
        out_specs=pl.BlockSpec((bsz, tn), lambda j: (0, j)),
        out_shape=jax.ShapeDtypeStruct((bsz, n), F32),
        compiler_params=pltpu.CompilerParams(vmem_limit_bytes=VMEM_LIMIT),
        name="ada",
    )(c, w_ada, b_ada.reshape(1, n))


_SECTION_KIND = dict(qa="qk", ka="ka", va="va", ga="gate", qb="qk", kb="qk", vb="v", gb="gate")
_SECTION_WIDTH = dict(qa=512, ka=128, va=128, ga=512, qb=512, kb=512, vb=512, gb=512)
MM_WIDTH = 1024
EPI_WIDTH = 256


def _matmul_chunks():
    pieces = []
    for name in sorted(SRC, key=SRC.get):
        for off in range(0, _SECTION_WIDTH[name], EPI_WIDTH):
            w = min(EPI_WIDTH, _SECTION_WIDTH[name] - off)
            dst = off // LANES if name == "vb" else DST[name] + off
            pieces.append((_SECTION_KIND[name], SRC[name] + off, w, dst))
    chunks = []
    for kind, src, w, dst in pieces:
        if chunks and src + w - chunks[-1][0] <= MM_WIDTH:
            chunks[-1][2].append((kind, src - chunks[-1][0], w, dst))
            chunks[-1][1] = src + w - chunks[-1][0]
        else:
            chunks.append([src, w, [(kind, 0, w, dst)]])
    return tuple((s, w, tuple(p)) for s, w, p in chunks)


_MATMUL_CHUNKS = _matmul_chunks()


def _rope_kernel(pos_ref, invf_ref, cos_ref, sin_ref):
    ang = invf_ref[...] * pos_ref[0].astype(F32)
    cos = jnp.cos(ang)
    sin = jnp.sin(ang)
    cos4 = jnp.concatenate([cos, cos, cos, cos], axis=0)
    sin4 = jnp.concatenate([-sin, sin, -sin, sin], axis=0)
    cos_ref[0] = cos4.T
    sin_ref[0] = sin4.T


def _rope_tables(positions, inv_freq):
    bsz = positions.shape[0]
    nfreq = HEAD_DIM // 2
    return pl.pallas_call(
        _rope_kernel,
        grid=(bsz,),
        in_specs=[pl.BlockSpec((1, 1, SEQ), lambda b: (b, 0, 0)),
                  pl.BlockSpec((nfreq, 1), lambda b: (0, 0))],
        out_specs=[pl.BlockSpec((1, SEQ, LANES), lambda b: (b, 0, 0)),
                   pl.BlockSpec((1, SEQ, LANES), lambda b: (b, 0, 0))],
        out_shape=[jax.ShapeDtypeStruct((bsz, SEQ, LANES), F32)] * 2,
        compiler_params=pltpu.CompilerParams(
            dimension_semantics=("parallel",), vmem_limit_bytes=VMEM_LIMIT),
        name="rope",
    )(positions.reshape(bsz, 1, SEQ), inv_freq.reshape(nfreq, 1))


def _inproj_kernel(x_ref, cos_ref, sin_ref, mod_ref, ng_ref, w_ref, cg_ref, ones_ref,
                   o_ref, vt_ref, h_scr):
    x = x_ref[0]
    ms = jnp.mean(x * x, axis=-1, keepdims=True)
    xn = x * lax.rsqrt(ms + EPS)
    shift = mod_ref[0, 0:1, :]
    scale = mod_ref[0, 1:2, :]
    h = (xn * ng_ref[...]) * (1.0 + scale) + shift
    h_scr[...] = h.astype(BF16)

    cos = cos_ref[0]
    sin = sin_ref[0]
    lane = lax.broadcasted_iota(jnp.int32, cos.shape, 1)
    first_half = (lane & (HEAD_DIM - 1)) < HEAD_DIM // 2
    low_head = lane < HEAD_DIM

    for src0, width, pieces in _MATMUL_CHUNKS:
        pm = jnp.dot(h_scr[...], w_ref[:, src0:src0 + width], preferred_element_type=F32)
        for kind, off, w, dst in pieces:
            src = src0 + off
            p = pm[:, off:off + w]
            if kind == "va":
                p_sw = pltpu.roll(p, HEAD_DIM, 1)
                o_ref[0, :, dst:dst + LANES] = jnp.where(low_head, p, p_sw).astype(BF16)
                o_ref[0, :, dst + LANES:dst + 2 * LANES] = jnp.where(low_head, p_sw, p).astype(BF16)
            elif kind == "v":
                p_t = p.T
                for n in range(w // LANES):
                    r0 = (dst + n) * VT_ROWS
                    vt_ref[0, r0:r0 + LANES, :] = p_t[n * LANES:(n + 1) * LANES, :].astype(BF16)
                    vt_ref[0, r0 + LANES:r0 + VT_ROWS, :] = jnp.ones((VT_ROWS - LANES, p_t.shape[1]), BF16)
            elif kind == "gate":
                o_ref[0, :, dst:dst + w] = (p * jax.nn.sigmoid(p)).astype(BF16)
            else:
                ss = jnp.dot((p * p).astype(BF16), ones_ref[0:w, 0:w], preferred_element_type=F32)
                pn = p * lax.rsqrt(ss * (1.0 / HEAD_DIM) + EPS) * cg_ref[:, src:src + w]
                for j in range(w // LANES):
                    t = pn[:, j * LANES:(j + 1) * LANES]
                    rot = jnp.where(first_half, pltpu.roll(t, LANES - HEAD_DIM // 2, 1),
                                    pltpu.roll(t, HEAD_DIM // 2, 1))
                    r = t * cos + rot * sin
                    if kind == "ka":
                        r_sw = pltpu.roll(r, HEAD_DIM, 1)
                        variants = (jnp.where(low_head, r, 0.0), jnp.where(low_head, 0.0, r_sw),
                                    jnp.where(low_head, r_sw, 0.0), jnp.where(low_head, 0.0, r))
                        for n, kvar in enumerate(variants):
                            o_ref[0, :, dst + n * LANES:dst + (n + 1) * LANES] = kvar.astype(BF16)
                    else:
                        o_ref[0, :, dst + j * LANES:dst + (j + 1) * LANES] = r.astype(BF16)


def _inproj(x, cos_t, sin_t, mod3, norm_gain, w_in_bf, colgain, ones_blk):
    bsz = x.shape[0]
    return pl.pallas_call(
        _inproj_kernel,
        grid=(bsz, SEQ // TM_IN),
        in_specs=[pl.BlockSpec((1, TM_IN, D_MODEL), lambda b, i: (b, i, 0)),
                  pl.BlockSpec((1, TM_IN, LANES), lambda b, i: (b, i, 0)),
                  pl.BlockSpec((1, TM_IN, LANES), lambda b, i: (b, i, 0)),
                  pl.BlockSpec((1, 3, D_MODEL), lambda b, i: (b, 0, 0)),
                  pl.BlockSpec((1, D_MODEL), lambda b, i: (0, 0)),
                  pl.BlockSpec((D_MODEL, IN_WIDTH), lambda b, i: (0, 0)),
                  pl.BlockSpec((1, IN_WIDTH), lambda b, i: (0, 0)),
                  pl.BlockSpec((256, 256), lambda b, i: (0, 0))],
        out_specs=[pl.BlockSpec((1, TM_IN, PROJ_WIDTH), lambda b, i: (b, i, 0)),
                   pl.BlockSpec((1, B_HEADS * VT_ROWS, TM_IN), lambda b, i: (b, 0, i))],
        out_shape=[jax.ShapeDtypeStruct((bsz, SEQ, PROJ_WIDTH), BF16),
                   jax.ShapeDtypeStruct((bsz, B_HEADS * VT_ROWS, SEQ), BF16)],
        scratch_shapes=[pltpu.VMEM((TM_IN, D_MODEL), BF16)],
        compiler_params=pltpu.CompilerParams(
            dimension_semantics=("parallel", "parallel"), vmem_limit_bytes=VMEM_LIMIT),
        name="inproj",
    )(x, cos_t, sin_t, mod3, norm_gain, w_in_bf, colgain, ones_blk)


def _attn_a_out_kernel(sink_ref, q_ref, ke0_ref, ko0_ref, ke1_ref, ko1_ref, vd0_ref, vd1_ref, g_ref,
                       x_ref, yb_ref, wo_ref, mod_ref, o_ref, ya_scr):
    i = pl.program_id(1)
    k_refs = ((ke0_ref, ko0_ref), (ke1_ref, ko1_ref))
    v_refs = (vd0_ref, vd1_ref)
    row_minus_col = ((lax.broadcasted_iota(jnp.int32, (2 * BLOCK, SPAN), 0) & (BLOCK - 1))
                     - lax.broadcasted_iota(jnp.int32, (2 * BLOCK, SPAN), 1))
    lo_out = lax.broadcasted_iota(jnp.int32, (BLOCK, LANES), 1) < HEAD_DIM
    first_group = lax.broadcasted_iota(jnp.int32, (2 * BLOCK, 1), 0) < BLOCK
    sinks = [[jnp.where(first_group, sink_ref[4 * kv + parity], sink_ref[4 * kv + 2 + parity]) * LOG2E
              for parity in range(2)] for kv in range(2)]

    def scores(j):
        blk = i * NB_A + j
        start = pl.multiple_of(jnp.clip(blk * BLOCK - WINDOW, 0, SEQ - SPAN), BLOCK)
        valid = jnp.abs(row_minus_col + (blk * BLOCK - start)) <= WINDOW
        rows = slice(j * BLOCK, (j + 1) * BLOCK)
        out = []
        for kv in range(2):
            c0 = 2 * kv * LANES
            q2 = jnp.concatenate([q_ref[0, rows, c0:c0 + LANES], q_ref[0, rows, c0 + LANES:c0 + 2 * LANES]],
                                 axis=0)
            for parity in range(2):
                s = _nt_dot(q2, k_refs[kv][parity][0, pl.ds(start, SPAN), :])
                out.append(jnp.where(valid, s, NEG))
        return start, out

    nxt = scores(0)
    for j in range(NB_A):
        start, s_list = nxt
        if j + 1 < NB_A:
            nxt = scores(j + 1)
        rows = slice(j * BLOCK, (j + 1) * BLOCK)
        for kv in range(2):
            c0 = 2 * kv * LANES
            outs = []
            for parity in range(2):
                s = s_list[2 * kv + parity]
                sink = sinks[kv][parity]
                m = jnp.maximum(jnp.max(s, axis=-1, keepdims=True), sink)
                p = jnp.exp2(s - m)
                denom = jnp.sum(p, axis=-1, keepdims=True) + jnp.exp2(sink - m)
                o = jnp.dot(p.astype(BF16), v_refs[kv][0, pl.ds(start, SPAN), :],
                            preferred_element_type=F32)
                outs.append(o / denom)
            for grp in range(2):
                grp_rows = slice(grp * BLOCK, (grp + 1) * BLOCK)
                y = jnp.where(lo_out, outs[0][grp_rows], outs[1][grp_rows])
                cols = slice(c0 + grp * LANES, c0 + (grp + 1) * LANES)
                ya_scr[rows, cols] = (y * g_ref[0, rows, cols].astype(F32)).astype(BF16)
        if (j + 1) % OUT_BLOCKS_A == 0:
            out_rows = slice((j + 1 - OUT_BLOCKS_A) * BLOCK, (j + 1) * BLOCK)
            y_out = (jnp.dot(ya_scr[out_rows, :], wo_ref[0:A_WIDTH, :], preferred_element_type=F32)
                     + jnp.dot(yb_ref[0, out_rows, :], wo_ref[A_WIDTH:, :], preferred_element_type=F32))
            o_ref[0, out_rows, :] = x_ref[0, out_rows, :] + mod_ref[0, 2:3, :] * y_out


def _attn_a_out(sink, proj, x, yb, w_out_bf, mod3):
    bsz = proj.shape[0]
    tq = NB_A * BLOCK

    def kv_spec(col):
        return pl.BlockSpec((1, SEQ, LANES), lambda b, i: (b, 0, col // LANES))

    return pl.pallas_call(
        _attn_a_out_kernel,
        grid=(bsz, SEQ // tq),
        in_specs=([pl.BlockSpec(memory_space=pltpu.SMEM),
                   pl.BlockSpec((1, tq, A_WIDTH), lambda b, i: (b, i, DST["qa"] // A_WIDTH))]
                  + [kv_spec(DST["ka"] + n * LANES) for n in range(KA_VARIANTS)]
                  + [kv_spec(DST["va"] + n * LANES) for n in range(VA_VARIANTS)]
                  + [pl.BlockSpec((1, tq, A_WIDTH), lambda b, i: (b, i, DST["ga"] // A_WIDTH)),
                     pl.BlockSpec((1, tq, D_MODEL), lambda b, i: (b, i, 0)),
                     pl.BlockSpec((1, tq, B_WIDTH), lambda b, i: (b, i, 0)),
                     pl.BlockSpec((A_WIDTH + B_WIDTH, D_MODEL), lambda b, i: (0, 0)),
                     pl.BlockSpec((1, 3, D_MODEL), lambda b, i: (b, 0, 0))]),
        out_specs=pl.BlockSpec((1, tq, D_MODEL), lambda b, i: (b, i, 0)),
        out_shape=jax.ShapeDtypeStruct((bsz, SEQ, D_MODEL), F32),
        scratch_shapes=[pltpu.VMEM((tq, A_WIDTH), BF16)],
        compiler_params=pltpu.CompilerParams(
            dimension_semantics=("parallel", "parallel"), vmem_limit_bytes=VMEM_LIMIT),
        name="attn_a_out",
    )(sink, *([proj] * (2 + KA_VARIANTS + VA_VARIANTS)), x, yb, w_out_bf, mod3)


def _attn_b_kernel(lam_ref, q_ref, k_ref, vt_ref, g_ref, sg_ref, o_ref):
    lq1, lk1, lq2, lk2 = (lam_ref[r:r + 1, :] for r in range(4))
    lam = (jnp.exp(jnp.sum(lq1 * lk1, axis=-1, keepdims=True))
           - jnp.exp(jnp.sum(lq2 * lk2, axis=-1, keepdims=True)) + LAMBDA_INIT)
    lo = lax.broadcasted_iota(jnp.int32, (TQ_B, LANES), 1) < HEAD_DIM

    units = [(qt, h) for qt in range(QT_B) for h in range(B_HEADS)]

    def scores_t(unit):
        qt, h = unit
        cols = slice(h * LANES, (h + 1) * LANES)
        qp = q_ref[0, qt * TQ_B:(qt + 1) * TQ_B, cols].astype(F32)
        q01 = jnp.concatenate([jnp.where(lo, qp, 0.0), jnp.where(lo, 0.0, qp)], axis=0).astype(BF16)
        return _nt_dot(k_ref[0, :, cols], q01)

    ahead = [scores_t(u) for u in units[:QK_AHEAD_B]]
    for n, (qt, h) in enumerate(units):
        cols = slice(h * LANES, (h + 1) * LANES)
        rows = slice(qt * TQ_B, (qt + 1) * TQ_B)
        s_t = ahead.pop(0)
        if n + QK_AHEAD_B < len(units):
            ahead.append(scores_t(units[n + QK_AHEAD_B]))
        m = jnp.max(s_t, axis=0, keepdims=True)
        e_t = jnp.exp2(s_t - m).astype(BF16)
        ol = jnp.dot(vt_ref[0, h * VT_ROWS:(h + 1) * VT_ROWS, :], e_t,
                     preferred_element_type=F32)
        o0 = ol[:LANES, :TQ_B] / ol[LANES:LANES + 1, :TQ_B]
        o1 = ol[:LANES, TQ_B:] / ol[LANES:LANES + 1, TQ_B:]
        o = (o0 - lam * o1).T
        on = o * lax.rsqrt(jnp.mean(o * o, axis=-1, keepdims=True) + EPS)
        on = on * sg_ref[...] * (1.0 - LAMBDA_INIT)
        o_ref[0, rows, cols] = (on * g_ref[0, rows, cols].astype(F32)).astype(BF16)


def _attn_b(lam_params, proj, vt, subln_gain):
    bsz = proj.shape[0]
    tq = QT_B * TQ_B
    return pl.pallas_call(
        _attn_b_kernel,
        grid=(bsz, SEQ // tq),
        in_specs=[pl.BlockSpec((4, HEAD_DIM), lambda b, i: (0, 0)),
                  pl.BlockSpec((1, tq, B_WIDTH), lambda b, i: (b, i, DST["qb"] // B_WIDTH)),
                  pl.BlockSpec((1, SEQ, B_WIDTH), lambda b, i: (b, 0, DST["kb"] // B_WIDTH)),
                  pl.BlockSpec((1, B_HEADS * VT_ROWS, SEQ), lambda b, i: (b, 0, 0)),
                  pl.BlockSpec((1, tq, B_WIDTH), lambda b, i: (b, i, DST["gb"] // B_WIDTH)),
                  pl.BlockSpec((1, LANES), lambda b, i: (0, 0))],
        out_specs=pl.BlockSpec((1, tq, B_WIDTH), lambda b, i: (b, i, 0)),
        out_shape=jax.ShapeDtypeStruct((bsz, SEQ, B_WIDTH), BF16),
        compiler_params=pltpu.CompilerParams(
            dimension_semantics=("parallel", "parallel"), vmem_limit_bytes=VMEM_LIMIT),
        name="attn_b",
    )(lam_params, proj, proj, vt, proj, subln_gain)


def _column_gains(q_norm_a, k_norm_a, q_norm_b, k_norm_b):
    g = jnp.ones((IN_WIDTH,), F32)
    g = g.at[SRC["qa"]:SRC["qa"] + 512].set(jnp.tile(q_norm_a.astype(F32), 8) * Q_PRESCALE)
    g = g.at[SRC["ka"]:SRC["ka"] + 128].set(jnp.tile(k_norm_a.astype(F32), 2))
    g = g.at[SRC["qb"]:SRC["qb"] + 512].set(jnp.tile(q_norm_b.astype(F32), 8) * Q_PRESCALE)
    g = g.at[SRC["kb"]:SRC["kb"] + 512].set(jnp.tile(k_norm_b.astype(F32), 8))
    return g.reshape(1, IN_WIDTH)


def kernel(x, c, positions, w_ada, b_ada, norm_gain, w_in, q_norm_a, k_norm_a, sink_a, q_norm_b,
           k_norm_b, lambda_q1, lambda_k1, lambda_q2, lambda_k2, subln_gain, w_out):
    assert w_ada.shape[0] == 1, "single-layer trunk"
    bsz = x.shape[0]
    inv_freq = 1.0 / (ROPE_THETA ** (jnp.arange(0, HEAD_DIM, 2, dtype=F32) / HEAD_DIM))
    ones_blk = jnp.asarray(np.kron(np.eye(4), np.ones((HEAD_DIM, HEAD_DIM))), dtype=BF16)
    colgain = _column_gains(q_norm_a[0], k_norm_a[0], q_norm_b[0], k_norm_b[0])
    lam_params = jnp.concatenate([lambda_q1, lambda_k1, lambda_q2, lambda_k2], axis=0).astype(F32)

    mod3 = _ada(c, w_ada[0], b_ada[0]).reshape(bsz, 3, D_MODEL)
    cos_t, sin_t = _rope_tables(positions, inv_freq)
    proj, vt = _inproj(x, cos_t, sin_t, mod3, norm_gain, w_in[0].astype(BF16), colgain, ones_blk)
    yb = _attn_b(lam_params, proj, vt, subln_gain)
    return _attn_a_out(sink_a[0].astype(F32), proj, x, yb, w_out[0].astype(BF16), mod3)
```

```python
import functools
import math

import jax
import jax.numpy as jnp
import numpy as np
from jax import lax
from jax.experimental import pallas as pl
from jax.experimental.pallas import tpu as pltpu

F32 = jnp.float32
BF16 = jnp.bfloat16

D_MODEL = 1024
SEQ = 2048
HEAD_DIM = 64
LANES = 128
A_WIDTH = 512
A_KV_WIDTH = 128
B_WIDTH = 512
B_HEADS = 4
WINDOW = 128
BLOCK = 128
SPAN = BLOCK + 2 * WINDOW
ROPE_THETA = 10000.0
EPS = 1e-6
LOG2E = math.log2(math.e)
Q_PRESCALE = LOG2E / math.sqrt(HEAD_DIM)
NEG = -1e30
SAFE_LOG2_SCORE = 64.0
LAMBDA_INIT = 0.8 - 0.6 * math.exp(-0.3 * 0)

SRC = dict(qa=0, ka=512, va=640, ga=768, qb=1280, kb=1792, vb=2304, gb=2816)
IN_WIDTH = 3328
DST = dict(qa=0, ga=512, qb=1024, kb=1536, gb=2048, ka=2560, va=3072)
KA_VARIANTS = 4
VA_VARIANTS = 2
PROJ_WIDTH = 3328
VT_ROWS = 144

VMEM_LIMIT = 48 * 1024 * 1024
TM_IN = 512
NB_A = 8
TQ_B = 512
QT_B = 4
QK_AHEAD_B = 1
OUT_BLOCKS_A = 4


def _nt_dot(a, b):
    return lax.dot_general(a, b, (((1,), (1,)), ((), ())), preferred_element_type=F32)


def _ada_kernel(c_ref, w_ref, b_ref, o_ref):
    c = c_ref[...]
    sc = c * jax.nn.sigmoid(c)
    o_ref[...] = jnp.dot(sc.astype(BF16), w_ref[...].astype(BF16),
                         preferred_element_type=F32) + b_ref[...]


def _ada(c, w_ada, b_ada):
    bsz = c.shape[0]
    n = w_ada.shape[1]
    tn = 1024
    return pl.pallas_call(
        _ada_kernel,
        grid=(n // tn,),
        in_specs=[pl.BlockSpec((bsz, D_MODEL), lambda j: (0, 0)),
                  pl.BlockSpec((D_MODEL, tn), lambda j: (0, j)),
                  pl.BlockSpec((1, tn), lambda j: (0, j))],
        out_specs=pl.BlockSpec((bsz, tn), lambda j: (0, j)),
        out_shape=jax.ShapeDtypeStruct((bsz, n), F32),
        compiler_params=pltpu.CompilerParams(vmem_limit_bytes=VMEM_LIMIT),
        name="ada",
    )(c, w_ada, b_ada.reshape(1, n))


_SECTION_KIND = dict(qa="qk", ka="ka", va="va", ga="gate", qb="qk", kb="qk", vb="v", gb="gate")
_SECTION_WIDTH = dict(qa=512, ka=128, va=128, ga=512, qb=512, kb=512, vb=512, gb=512)
MM_WIDTH = 1024
EPI_WIDTH = 256


def _matmul_chunks():
    pieces = []
    for name in sorted(SRC, key=SRC.get):
        for off in range(0, _SECTION_WIDTH[name], EPI_WIDTH):
            w = min(EPI_WIDTH, _SECTION_WIDTH[name] - off)
            dst = off // LANES if name == "vb" else DST[name] + off
            pieces.append((_SECTION_KIND[name], SRC[name] + off, w, dst))
    chunks = []
    for kind, src, w, dst in pieces:
        if chunks and src + w - chunks[-1][0] <= MM_WIDTH:
            chunks[-1][2].append((kind, src - chunks[-1][0], w, dst))
            chunks[-1][1] = src + w - chunks[-1][0]
        else:
            chunks.append([src, w, [(kind, 0, w, dst)]])
    return tuple((s, w, tuple(p)) for s, w, p in chunks)


_MATMUL_CHUNKS = _matmul_chunks()


def _rope_kernel(pos_ref, invf_ref, cos_ref, sin_ref):
    ang = invf_ref[...] * pos_ref[0].astype(F32)
    cos = jnp.cos(ang)
    sin = jnp.sin(ang)
    cos4 = jnp.concatenate([cos, cos, cos, cos], axis=0)
    sin4 = jnp.concatenate([-sin, sin, -sin, sin], axis=0)
    cos_ref[0] = cos4.T
    sin_ref[0] = sin4.T


def _rope_tables(positions, inv_freq):
    bsz = positions.shape[0]
    nfreq = HEAD_DIM // 2
    return pl.pallas_call(
        _rope_kernel,
        grid=(bsz,),
        in_specs=[pl.BlockSpec((1, 1, SEQ), lambda b: (b, 0, 0)),
                  pl.BlockSpec((nfreq, 1), lambda b: (0, 0))],
        out_specs=[pl.BlockSpec((1, SEQ, LANES), lambda b: (b, 0, 0)),
                   pl.BlockSpec((1, SEQ, LANES), lambda b: (b, 0, 0))],
        out_shape=[jax.ShapeDtypeStruct((bsz, SEQ, LANES), F32)] * 2,
        compiler_params=pltpu.CompilerParams(
            dimension_semantics=("parallel",), vmem_limit_bytes=VMEM_LIMIT),
        name="rope",
    )(positions.reshape(bsz, 1, SEQ), inv_freq.reshape(nfreq, 1))


def _inproj_kernel(x_ref, cos_ref, sin_ref, mod_ref, ng_ref, w_ref, cg_ref, ones_ref,
                   o_ref, vt_ref, h_scr):
    x = x_ref[0]
    ms = jnp.mean(x * x, axis=-1, keepdims=True)
    xn = x * lax.rsqrt(ms + EPS)
    shift = mod_ref[0, 0:1, :]
    scale = mod_ref[0, 1:2, :]
    h = (xn * ng_ref[...]) * (1.0 + scale) + shift
    h_scr[...] = h.astype(BF16)

    cos = cos_ref[0]
    sin = sin_ref[0]
    lane = lax.broadcasted_iota(jnp.int32, cos.shape, 1)
    first_half = (lane & (HEAD_DIM - 1)) < HEAD_DIM // 2
    low_head = lane < HEAD_DIM

    for src0, width, pieces in _MATMUL_CHUNKS:
        pm = jnp.dot(h_scr[...], w_ref[:, src0:src0 + width], preferred_element_type=F32)
        for kind, off, w, dst in pieces:
            src = src0 + off
            p = pm[:, off:off + w]
            if kind == "va":
                p_sw = pltpu.roll(p, HEAD_DIM, 1)
                o_ref[0, :, dst:dst + LANES] = jnp.where(low_head, p, p_sw).astype(BF16)
                o_ref[0, :, dst + LANES:dst + 2 * LANES] = jnp.where(low_head, p_sw, p).astype(BF16)
            elif kind == "v":
                p_t = p.T
                for n in range(w // LANES):
                    r0 = (dst + n) * VT_ROWS
                    vt_ref[0, r0:r0 + LANES, :] = p_t[n * LANES:(n + 1) * LANES, :].astype(BF16)
                    vt_ref[0, r0 + LANES:r0 + VT_ROWS, :] = jnp.ones((VT_ROWS - LANES, p_t.shape[1]), BF16)
            elif kind == "gate":
                o_ref[0, :, dst:dst + w] = (p * jax.nn.sigmoid(p)).astype(BF16)
            else:
                ss = jnp.dot((p * p).astype(BF16), ones_ref[0:w, 0:w], preferred_element_type=F32)
                pn = p * lax.rsqrt(ss * (1.0 / HEAD_DIM) + EPS) * cg_ref[:, src:src + w]
                for j in range(w // LANES):
                    t = pn[:, j * LANES:(j + 1) * LANES]
                    rot = jnp.where(first_half, pltpu.roll(t, LANES - HEAD_DIM // 2, 1),
                                    pltpu.roll(t, HEAD_DIM // 2, 1))
                    r = t * cos + rot * sin
                    if kind == "ka":
                        r_sw = pltpu.roll(r, HEAD_DIM, 1)
                        variants = (jnp.where(low_head, r, 0.0), jnp.where(low_head, 0.0, r_sw),
                                    jnp.where(low_head, r_sw, 0.0), jnp.where(low_head, 0.0, r))
                        for n, kvar in enumerate(variants):
                            o_ref[0, :, dst + n * LANES:dst + (n + 1) * LANES] = kvar.astype(BF16)
                    else:
                        o_ref[0, :, dst + j * LANES:dst + (j + 1) * LANES] = r.astype(BF16)


def _inproj(x, cos_t, sin_t, mod3, norm_gain, w_in_bf, colgain, ones_blk):
    bsz = x.shape[0]
    return pl.pallas_call(
        _inproj_kernel,
        grid=(bsz, SEQ // TM_IN),
        in_specs=[pl.BlockSpec((1, TM_IN, D_MODEL), lambda b, i: (b, i, 0)),
                  pl.BlockSpec((1, TM_IN, LANES), lambda b, i: (b, i, 0)),
                  pl.BlockSpec((1, TM_IN, LANES), lambda b, i: (b, i, 0)),
                  pl.BlockSpec((1, 3, D_MODEL), lambda b, i: (b, 0, 0)),
                  pl.BlockSpec((1, D_MODEL), lambda b, i: (0, 0)),
                  pl.BlockSpec((D_MODEL, IN_WIDTH), lambda b, i: (0, 0)),
                  pl.BlockSpec((1, IN_WIDTH), lambda b, i: (0, 0)),
                  pl.BlockSpec((256, 256), lambda b, i: (0, 0))],
        out_specs=[pl.BlockSpec((1, TM_IN, PROJ_WIDTH), lambda b, i: (b, i, 0)),
                   pl.BlockSpec((1, B_HEADS * VT_ROWS, TM_IN), lambda b, i: (b, 0, i))],
        out_shape=[jax.ShapeDtypeStruct((bsz, SEQ, PROJ_WIDTH), BF16),
                   jax.ShapeDtypeStruct((bsz, B_HEADS * VT_ROWS, SEQ), BF16)],
        scratch_shapes=[pltpu.VMEM((TM_IN, D_MODEL), BF16)],
        compiler_params=pltpu.CompilerParams(
            dimension_semantics=("parallel", "parallel"), vmem_limit_bytes=VMEM_LIMIT),
        name="inproj",
    )(x, cos_t, sin_t, mod3, norm_gain, w_in_bf, colgain, ones_blk)


def _attn_a_out_kernel(sink_ref, q_ref, ke0_ref, ko0_ref, ke1_ref, ko1_ref, vd0_ref, vd1_ref, g_ref,
                       x_ref, yb_ref, wo_ref, mod_ref, o_ref, ya_scr, *, bounded_scores):
    i = pl.program_id(1)
    k_refs = ((ke0_ref, ko0_ref), (ke1_ref, ko1_ref))
    v_refs = (vd0_ref, vd1_ref)
    row_minus_col = ((lax.broadcasted_iota(jnp.int32, (2 * BLOCK, SPAN), 0) & (BLOCK - 1))
                     - lax.broadcasted_iota(jnp.int32, (2 * BLOCK, SPAN), 1))
    lo_out = lax.broadcasted_iota(jnp.int32, (BLOCK, LANES), 1) < HEAD_DIM
    first_group = lax.broadcasted_iota(jnp.int32, (2 * BLOCK, 1), 0) < BLOCK
    sinks = [[jnp.where(first_group, sink_ref[4 * kv + parity], sink_ref[4 * kv + 2 + parity]) * LOG2E
              for parity in range(2)] for kv in range(2)]

    def scores(j):
        blk = i * NB_A + j
        start = pl.multiple_of(jnp.clip(blk * BLOCK - WINDOW, 0, SEQ - SPAN), BLOCK)
        valid = jnp.abs(row_minus_col + (blk * BLOCK - start)) <= WINDOW
        rows = slice(j * BLOCK, (j + 1) * BLOCK)
        out = []
        for kv in range(2):
            c0 = 2 * kv * LANES
            q2 = jnp.concatenate([q_ref[0, rows, c0:c0 + LANES], q_ref[0, rows, c0 + LANES:c0 + 2 * LANES]],
                                 axis=0)
            for parity in range(2):
                s = _nt_dot(q2, k_refs[kv][parity][0, pl.ds(start, SPAN), :])
                out.append(jnp.where(valid, s, NEG))
        return start, out

    nxt = scores(0)
    for j in range(NB_A):
        start, s_list = nxt
        if j + 1 < NB_A:
            nxt = scores(j + 1)
        rows = slice(j * BLOCK, (j + 1) * BLOCK)
        for kv in range(2):
            c0 = 2 * kv * LANES
            outs = []
            for parity in range(2):
                s = s_list[2 * kv + parity]
                sink = sinks[kv][parity]
                if bounded_scores:
                    p = jnp.exp2(s)
                    denom = jnp.sum(p, axis=-1, keepdims=True) + jnp.exp2(sink)
                else:
                    m = jnp.maximum(jnp.max(s, axis=-1, keepdims=True), sink)
                    p = jnp.exp2(s - m)
                    denom = jnp.sum(p, axis=-1, keepdims=True) + jnp.exp2(sink - m)
                o = jnp.dot(p.astype(BF16), v_refs[kv][0, pl.ds(start, SPAN), :],
                            preferred_element_type=F32)
                outs.append(o / denom)
            for grp in range(2):
                grp_rows = slice(grp * BLOCK, (grp + 1) * BLOCK)
                y = jnp.where(lo_out, outs[0][grp_rows], outs[1][grp_rows])
                cols = slice(c0 + grp * LANES, c0 + (grp + 1) * LANES)
                ya_scr[rows, cols] = (y * g_ref[0, rows, cols].astype(F32)).astype(BF16)
        if (j + 1) % OUT_BLOCKS_A == 0:
            out_rows = slice((j + 1 - OUT_BLOCKS_A) * BLOCK, (j + 1) * BLOCK)
            y_out = (jnp.dot(ya_scr[out_rows, :], wo_ref[0:A_WIDTH, :], preferred_element_type=F32)
                     + jnp.dot(yb_ref[0, out_rows, :], wo_ref[A_WIDTH:, :], preferred_element_type=F32))
            o_ref[0, out_rows, :] = x_ref[0, out_rows, :] + mod_ref[0, 2:3, :] * y_out


def _attn_a_out(sink, proj, x, yb, w_out_bf, mod3, bounded_scores):
    bsz = proj.shape[0]
    tq = NB_A * BLOCK

    def kv_spec(col):
        return pl.BlockSpec((1, SEQ, LANES), lambda b, i: (b, 0, col // LANES))

    return pl.pallas_call(
        functools.partial(_attn_a_out_kernel, bounded_scores=bounded_scores),
        grid=(bsz, SEQ // tq),
        in_specs=([pl.BlockSpec(memory_space=pltpu.SMEM),
                   pl.BlockSpec((1, tq, A_WIDTH), lambda b, i: (b, i, DST["qa"] // A_WIDTH))]
                  + [kv_spec(DST["ka"] + n * LANES) for n in range(KA_VARIANTS)]
                  + [kv_spec(DST["va"] + n * LANES) for n in range(VA_VARIANTS)]
                  + [pl.BlockSpec((1, tq, A_WIDTH), lambda b, i: (b, i, DST["ga"] // A_WIDTH)),
                     pl.BlockSpec((1, tq, D_MODEL), lambda b, i: (b, i, 0)),
                     pl.BlockSpec((1, tq, B_WIDTH), lambda b, i: (b, i, 0)),
                     pl.BlockSpec((A_WIDTH + B_WIDTH, D_MODEL), lambda b, i: (0, 0)),
                     pl.BlockSpec((1, 3, D_MODEL), lambda b, i: (b, 0, 0))]),
        out_specs=pl.BlockSpec((1, tq, D_MODEL), lambda b, i: (b, i, 0)),
        out_shape=jax.ShapeDtypeStruct((bsz, SEQ, D_MODEL), F32),
        scratch_shapes=[pltpu.VMEM((tq, A_WIDTH), BF16)],
        compiler_params=pltpu.CompilerParams(
            dimension_semantics=("parallel", "parallel"), vmem_limit_bytes=VMEM_LIMIT),
        name="attn_a_out",
    )(sink, *([proj] * (2 + KA_VARIANTS + VA_VARIANTS)), x, yb, w_out_bf, mod3)


def _attn_b_kernel(lam_ref, q_ref, k_ref, vt_ref, g_ref, sg_ref, o_ref, *, bounded_scores):
    lq1, lk1, lq2, lk2 = (lam_ref[r:r + 1, :] for r in range(4))
    lam = (jnp.exp(jnp.sum(lq1 * lk1, axis=-1, keepdims=True))
           - jnp.exp(jnp.sum(lq2 * lk2, axis=-1, keepdims=True)) + LAMBDA_INIT)
    lo = lax.broadcasted_iota(jnp.int32, (TQ_B, LANES), 1) < HEAD_DIM

    units = [(qt, h) for qt in range(QT_B) for h in range(B_HEADS)]

    def scores_t(unit):
        qt, h = unit
        cols = slice(h * LANES, (h + 1) * LANES)
        qp = q_ref[0, qt * TQ_B:(qt + 1) * TQ_B, cols].astype(F32)
        q01 = jnp.concatenate([jnp.where(lo, qp, 0.0), jnp.where(lo, 0.0, qp)], axis=0).astype(BF16)
        return _nt_dot(k_ref[0, :, cols], q01)

    ahead = [scores_t(u) for u in units[:QK_AHEAD_B]]
    for n, (qt, h) in enumerate(units):
        cols = slice(h * LANES, (h + 1) * LANES)
        rows = slice(qt * TQ_B, (qt + 1) * TQ_B)
        s_t = ahead.pop(0)
        if n + QK_AHEAD_B < len(units):
            ahead.append(scores_t(units[n + QK_AHEAD_B]))
        if bounded_scores:
            e_t = jnp.exp2(s_t).astype(BF16)
        else:
            m = jnp.max(s_t, axis=0, keepdims=True)
            e_t = jnp.exp2(s_t - m).astype(BF16)
        ol = jnp.dot(vt_ref[0, h * VT_ROWS:(h + 1) * VT_ROWS, :], e_t,
                     preferred_element_type=F32)
        o0 = ol[:LANES, :TQ_B] / ol[LANES:LANES + 1, :TQ_B]
        o1 = ol[:LANES, TQ_B:] / ol[LANES:LANES + 1, TQ_B:]
        o = (o0 - lam * o1).T
        on = o * lax.rsqrt(jnp.mean(o * o, axis=-1, keepdims=True) + EPS)
        on = on * sg_ref[...] * (1.0 - LAMBDA_INIT)
        o_ref[0, rows, cols] = (on * g_ref[0, rows, cols].astype(F32)).astype(BF16)


def _attn_b(lam_params, proj, vt, subln_gain, bounded_scores):
    bsz = proj.shape[0]
    tq = QT_B * TQ_B
    return pl.pallas_call(
        functools.partial(_attn_b_kernel, bounded_scores=bounded_scores),
        grid=(bsz, SEQ // tq),
        in_specs=[pl.BlockSpec((4, HEAD_DIM), lambda b, i: (0, 0)),
                  pl.BlockSpec((1, tq, B_WIDTH), lambda b, i: (b, i, DST["qb"] // B_WIDTH)),
                  pl.BlockSpec((1, SEQ, B_WIDTH), lambda b, i: (b, 0, DST["kb"] // B_WIDTH)),
                  pl.BlockSpec((1, B_HEADS * VT_ROWS, SEQ), lambda b, i: (b, 0, 0)),
                  pl.BlockSpec((1, tq, B_WIDTH), lambda b, i: (b, i, DST["gb"] // B_WIDTH)),
                  pl.BlockSpec((1, LANES), lambda b, i: (0, 0))],
        out_specs=pl.BlockSpec((1, tq, B_WIDTH), lambda b, i: (b, i, 0)),
        out_shape=jax.ShapeDtypeStruct((bsz, SEQ, B_WIDTH), BF16),
        compiler_params=pltpu.CompilerParams(
            dimension_semantics=("parallel", "parallel"), vmem_limit_bytes=VMEM_LIMIT),
        name="attn_b",
    )(lam_params, proj, proj, vt, proj, subln_gain)


def _column_gains(q_norm_a, k_norm_a, q_norm_b, k_norm_b):
    g = jnp.ones((IN_WIDTH,), F32)
    g = g.at[SRC["qa"]:SRC["qa"] + 512].set(jnp.tile(q_norm_a.astype(F32), 8) * Q_PRESCALE)
    g = g.at[SRC["ka"]:SRC["ka"] + 128].set(jnp.tile(k_norm_a.astype(F32), 2))
    g = g.at[SRC["qb"]:SRC["qb"] + 512].set(jnp.tile(q_norm_b.astype(F32), 8) * Q_PRESCALE)
    g = g.at[SRC["kb"]:SRC["kb"] + 512].set(jnp.tile(k_norm_b.astype(F32), 8))
    return g.reshape(1, IN_WIDTH)


def kernel(x, c, positions, w_ada, b_ada, norm_gain, w_in, q_norm_a, k_norm_a, sink_a, q_norm_b,
           k_norm_b, lambda_q1, lambda_k1, lambda_q2, lambda_k2, subln_gain, w_out):
    assert w_ada.shape[0] == 1, "single-layer trunk"
    bsz = x.shape[0]
    inv_freq = 1.0 / (ROPE_THETA ** (jnp.arange(0, HEAD_DIM, 2, dtype=F32) / HEAD_DIM))
    ones_blk = jnp.asarray(np.kron(np.eye(4), np.ones((HEAD_DIM, HEAD_DIM))), dtype=BF16)
    colgain = _column_gains(q_norm_a[0], k_norm_a[0], q_norm_b[0], k_norm_b[0])
    lam_params = jnp.concatenate([lambda_q1, lambda_k1, lambda_q2, lambda_k2], axis=0).astype(F32)

    mod3 = _ada(c, w_ada[0], b_ada[0]).reshape(bsz, 3, D_MODEL)
    cos_t, sin_t = _rope_tables(positions, inv_freq)
    proj, vt = _inproj(x, cos_t, sin_t, mod3, norm_gain, w_in[0].astype(BF16), colgain, ones_blk)
    def score_bound(q_gain, k_gain):
        return (HEAD_DIM * Q_PRESCALE * jnp.max(jnp.abs(q_gain.astype(F32)))
                * jnp.max(jnp.abs(k_gain.astype(F32))))

    yb = lax.cond(score_bound(q_norm_b[0], k_norm_b[0]) <= SAFE_LOG2_SCORE,
                  lambda: _attn_b(lam_params, proj, vt, subln_gain, True),
                  lambda: _attn_b(lam_params, proj, vt, subln_gain, False))
    sink = sink_a[0].astype(F32)
    w_out_bf = w_out[0].astype(BF16)
    bound_a = jnp.maximum(score_bound(q_norm_a[0], k_norm_a[0]), LOG2E * jnp.max(jnp.abs(sink)))
    return lax.cond(bound_a <= SAFE_LOG2_SCORE,
                    lambda: _attn_a_out(sink, proj, x, yb, w_out_bf, mod3, True),
                    lambda: _attn_a_out(sink, proj, x, yb, w_out_bf, mod3, False))
```

```python
import functools
import math

import jax
import jax.numpy as jnp
import numpy as np
from jax import lax
from jax.experimental import pallas as pl
from jax.experimental.pallas import tpu as pltpu

F32 = jnp.float32
BF16 = jnp.bfloat16

D_MODEL = 1024
SEQ = 2048
HEAD_DIM = 64
LANES = 128
A_WIDTH = 512
A_KV_WIDTH = 128
B_WIDTH = 512
B_HEADS = 4
WINDOW = 128
BLOCK = 128
SPAN = BLOCK + 2 * WINDOW
ROPE_THETA = 10000.0
EPS = 1e-6
LOG2E = math.log2(math.e)
Q_PRESCALE = LOG2E / math.sqrt(HEAD_DIM)
NEG = -1e30
SAFE_LOG2_SCORE = 64.0
LAMBDA_INIT = 0.8 - 0.6 * math.exp(-0.3 * 0)

SRC = dict(qa=0, ka=512, va=640, ga=768, qb=1280, kb=1792, vb=2304, gb=2816)
IN_WIDTH = 3328
DST = dict(qa=0, ga=512, qb=1024, kb=1536, gb=2048, ka=2560, va=3072)
KA_VARIANTS = 4
VA_VARIANTS = 4
PROJ_WIDTH = 3584
VT_ROWS = 144

VMEM_LIMIT = 48 * 1024 * 1024
TM_IN = 1024
NB_A = 8
TQ_B = 512
QT_B = 4
QK_AHEAD_B = 1
OUT_BLOCKS_A = 4


def _nt_dot(a, b):
    return lax.dot_general(a, b, (((1,), (1,)), ((), ())), preferred_element_type=F32)


def _ada_kernel(c_ref, w_ref, b_ref, o_ref):
    c = c_ref[...]
    sc = c * jax.nn.sigmoid(c)
    o_ref[...] = jnp.dot(sc.astype(BF16), w_ref[...].astype(BF16),
                         preferred_element_type=F32) + b_ref[...]


def _ada(c, w_ada, b_ada):
    bsz = c.shape[0]
    n = w_ada.shape[1]
    tn = 1024
    return pl.pallas_call(
        _ada_kernel,
        grid=(n // tn,),
        in_specs=[pl.BlockSpec((bsz, D_MODEL), lambda j: (0, 0)),
                  pl.BlockSpec((D_MODEL, tn), lambda j: (0, j)),
                  pl.BlockSpec((1, tn), lambda j: (0, j))],
        out_specs=pl.BlockSpec((bsz, tn), lambda j: (0, j)),
        out_shape=jax.ShapeDtypeStruct((bsz, n), F32),
        compiler_params=pltpu.CompilerParams(vmem_limit_bytes=VMEM_LIMIT),
        name="ada",
    )(c, w_ada, b_ada.reshape(1, n))


_SECTION_KIND = dict(qa="qk", ka="ka", va="va", ga="gate", qb="qk", kb="qk", vb="v", gb="gate")
_SECTION_WIDTH = dict(qa=512, ka=128, va=128, ga=512, qb=512, kb=512, vb=512, gb=512)
MM_WIDTH = 1024
EPI_WIDTH = 256


def _matmul_chunks():
    pieces = []
    for name in sorted(SRC, key=SRC.get):
        for off in range(0, _SECTION_WIDTH[name], EPI_WIDTH):
            w = min(EPI_WIDTH, _SECTION_WIDTH[name] - off)
            dst = off // LANES if name == "vb" else DST[name] + off
            pieces.append((_SECTION_KIND[name], SRC[name] + off, w, dst))
    chunks = []
    for kind, src, w, dst in pieces:
        if chunks and src + w - chunks[-1][0] <= MM_WIDTH:
            chunks[-1][2].append((kind, src - chunks[-1][0], w, dst))
            chunks[-1][1] = src + w - chunks[-1][0]
        else:
            chunks.append([src, w, [(kind, 0, w, dst)]])
    return tuple((s, w, tuple(p)) for s, w, p in chunks)


_MATMUL_CHUNKS = _matmul_chunks()


def _rope_kernel(pos_ref, invf_ref, cos_ref, sin_ref):
    ang = invf_ref[...] * pos_ref[0].astype(F32)
    cos = jnp.cos(ang)
    sin = jnp.sin(ang)
    cos4 = jnp.concatenate([cos, cos, cos, cos], axis=0)
    sin4 = jnp.concatenate([-sin, sin, -sin, sin], axis=0)
    cos_ref[0] = cos4.T
    sin_ref[0] = sin4.T


def _rope_tables(positions, inv_freq):
    bsz = positions.shape[0]
    nfreq = HEAD_DIM // 2
    return pl.pallas_call(
        _rope_kernel,
        grid=(bsz,),
        in_specs=[pl.BlockSpec((1, 1, SEQ), lambda b: (b, 0, 0)),
                  pl.BlockSpec((nfreq, 1), lambda b: (0, 0))],
        out_specs=[pl.BlockSpec((1, SEQ, LANES), lambda b: (b, 0, 0)),
                   pl.BlockSpec((1, SEQ, LANES), lambda b: (b, 0, 0))],
        out_shape=[jax.ShapeDtypeStruct((bsz, SEQ, LANES), F32)] * 2,
        compiler_params=pltpu.CompilerParams(
            dimension_semantics=("parallel",), vmem_limit_bytes=VMEM_LIMIT),
        name="rope",
    )(positions.reshape(bsz, 1, SEQ), inv_freq.reshape(nfreq, 1))


def _inproj_kernel(x_ref, cos_ref, sin_ref, mod_ref, ng_ref, w_ref, cg_ref, ones_ref,
                   o_ref, vt_ref, h_scr):
    x = x_ref[0]
    ms = jnp.mean(x * x, axis=-1, keepdims=True)
    xn = x * lax.rsqrt(ms + EPS)
    shift = mod_ref[0, 0:1, :]
    scale = mod_ref[0, 1:2, :]
    h = (xn * ng_ref[...]) * (1.0 + scale) + shift
    h_scr[...] = h.astype(BF16)

    cos = cos_ref[0]
    sin = sin_ref[0]
    lane = lax.broadcasted_iota(jnp.int32, cos.shape, 1)
    first_half = (lane & (HEAD_DIM - 1)) < HEAD_DIM // 2
    low_head = lane < HEAD_DIM

    for src0, width, pieces in _MATMUL_CHUNKS:
        pm = jnp.dot(h_scr[...], w_ref[:, src0:src0 + width], preferred_element_type=F32)
        for kind, off, w, dst in pieces:
            src = src0 + off
            p = pm[:, off:off + w]
            if kind == "va":
                p_sw = pltpu.roll(p, HEAD_DIM, 1)
                variants = (jnp.where(low_head, p, 0.0), jnp.where(low_head, 0.0, p_sw),
                            jnp.where(low_head, p_sw, 0.0), jnp.where(low_head, 0.0, p))
                for n, vvar in enumerate(variants):
                    o_ref[0, :, dst + n * LANES:dst + (n + 1) * LANES] = vvar.astype(BF16)
            elif kind == "v":
                p_t = p.T
                for n in range(w // LANES):
                    r0 = (dst + n) * VT_ROWS
                    vt_ref[0, r0:r0 + LANES, :] = p_t[n * LANES:(n + 1) * LANES, :].astype(BF16)
                    vt_ref[0, r0 + LANES:r0 + VT_ROWS, :] = jnp.ones((VT_ROWS - LANES, p_t.shape[1]), BF16)
            elif kind == "gate":
                o_ref[0, :, dst:dst + w] = (p * jax.nn.sigmoid(p)).astype(BF16)
            else:
                ss = jnp.dot((p * p).astype(BF16), ones_ref[0:w, 0:w], preferred_element_type=F32)
                pn = p * lax.rsqrt(ss * (1.0 / HEAD_DIM) + EPS) * cg_ref[:, src:src + w]
                for j in range(w // LANES):
                    t = pn[:, j * LANES:(j + 1) * LANES]
                    rot = jnp.where(first_half, pltpu.roll(t, LANES - HEAD_DIM // 2, 1),
                                    pltpu.roll(t, HEAD_DIM // 2, 1))
                    r = t * cos + rot * sin
                    if kind == "ka":
                        r_sw = pltpu.roll(r, HEAD_DIM, 1)
                        variants = (jnp.where(low_head, r, 0.0), jnp.where(low_head, 0.0, r_sw),
                                    jnp.where(low_head, r_sw, 0.0), jnp.where(low_head, 0.0, r))
                        for n, kvar in enumerate(variants):
                            o_ref[0, :, dst + n * LANES:dst + (n + 1) * LANES] = kvar.astype(BF16)
                    else:
                        o_ref[0, :, dst + j * LANES:dst + (j + 1) * LANES] = r.astype(BF16)


def _inproj(x, cos_t, sin_t, mod3, norm_gain, w_in_bf, colgain, ones_blk):
    bsz = x.shape[0]
    return pl.pallas_call(
        _inproj_kernel,
        grid=(bsz, SEQ // TM_IN),
        in_specs=[pl.BlockSpec((1, TM_IN, D_MODEL), lambda b, i: (b, i, 0)),
                  pl.BlockSpec((1, TM_IN, LANES), lambda b, i: (b, i, 0)),
                  pl.BlockSpec((1, TM_IN, LANES), lambda b, i: (b, i, 0)),
                  pl.BlockSpec((1, 3, D_MODEL), lambda b, i: (b, 0, 0)),
                  pl.BlockSpec((1, D_MODEL), lambda b, i: (0, 0)),
                  pl.BlockSpec((D_MODEL, IN_WIDTH), lambda b, i: (0, 0)),
                  pl.BlockSpec((1, IN_WIDTH), lambda b, i: (0, 0)),
                  pl.BlockSpec((256, 256), lambda b, i: (0, 0))],
        out_specs=[pl.BlockSpec((1, TM_IN, PROJ_WIDTH), lambda b, i: (b, i, 0)),
                   pl.BlockSpec((1, B_HEADS * VT_ROWS, TM_IN), lambda b, i: (b, 0, i))],
        out_shape=[jax.ShapeDtypeStruct((bsz, SEQ, PROJ_WIDTH), BF16),
                   jax.ShapeDtypeStruct((bsz, B_HEADS * VT_ROWS, SEQ), BF16)],
        scratch_shapes=[pltpu.VMEM((TM_IN, D_MODEL), BF16)],
        compiler_params=pltpu.CompilerParams(
            dimension_semantics=("parallel", "parallel"), vmem_limit_bytes=VMEM_LIMIT),
        name="inproj",
    )(x, cos_t, sin_t, mod3, norm_gain, w_in_bf, colgain, ones_blk)


def _attn_a_out_kernel(sink_ref, q_ref, ke0_ref, ko0_ref, ke1_ref, ko1_ref, ve0_ref, vo0_ref, ve1_ref,
                       vo1_ref, g_ref, x_ref, yb_ref, wo_ref, mod_ref, o_ref, ya_scr, *, bounded_scores):
    i = pl.program_id(1)
    k_refs = ((ke0_ref, ko0_ref), (ke1_ref, ko1_ref))
    v_refs = ((ve0_ref, vo0_ref), (ve1_ref, vo1_ref))
    row_minus_col = ((lax.broadcasted_iota(jnp.int32, (2 * BLOCK, SPAN), 0) & (BLOCK - 1))
                     - lax.broadcasted_iota(jnp.int32, (2 * BLOCK, SPAN), 1))
    lo_out = lax.broadcasted_iota(jnp.int32, (2 * BLOCK, LANES), 1) < HEAD_DIM
    first_group = lax.broadcasted_iota(jnp.int32, (2 * BLOCK, 1), 0) < BLOCK
    sinks = [[jnp.where(first_group, sink_ref[4 * kv + parity], sink_ref[4 * kv + 2 + parity]) * LOG2E
              for parity in range(2)] for kv in range(2)]

    def both_parities(refs, start):
        return jnp.concatenate([r[0, pl.ds(start, SPAN), :] for r in refs], axis=0)

    def scores(j):
        blk = i * NB_A + j
        start = pl.multiple_of(jnp.clip(blk * BLOCK - WINDOW, 0, SEQ - SPAN), BLOCK)
        valid = jnp.abs(row_minus_col + (blk * BLOCK - start)) <= WINDOW
        valid = jnp.concatenate([valid, valid], axis=1)
        rows = slice(j * BLOCK, (j + 1) * BLOCK)
        out = []
        for kv in range(2):
            c0 = 2 * kv * LANES
            q2 = jnp.concatenate([q_ref[0, rows, c0:c0 + LANES], q_ref[0, rows, c0 + LANES:c0 + 2 * LANES]],
                                 axis=0)
            s = _nt_dot(q2, both_parities(k_refs[kv], start))
            out.append(jnp.where(valid, s, NEG))
        return start, out

    nxt = scores(0)
    for j in range(NB_A):
        start, s_list = nxt
        if j + 1 < NB_A:
            nxt = scores(j + 1)
        rows = slice(j * BLOCK, (j + 1) * BLOCK)
        for kv in range(2):
            c0 = 2 * kv * LANES
            ps, denoms = [], []
            for parity in range(2):
                s = s_list[kv][:, parity * SPAN:(parity + 1) * SPAN]
                sink = sinks[kv][parity]
                if bounded_scores:
                    p = jnp.exp2(s)
                    denom = jnp.sum(p, axis=-1, keepdims=True) + jnp.exp2(sink)
                else:
                    m = jnp.maximum(jnp.max(s, axis=-1, keepdims=True), sink)
                    p = jnp.exp2(s - m)
                    denom = jnp.sum(p, axis=-1, keepdims=True) + jnp.exp2(sink - m)
                ps.append(p.astype(BF16))
                denoms.append(denom)
            o = jnp.dot(jnp.concatenate(ps, axis=1), both_parities(v_refs[kv], start),
                        preferred_element_type=F32)
            y = o / jnp.where(lo_out, denoms[0], denoms[1])
            for grp in range(2):
                grp_rows = slice(grp * BLOCK, (grp + 1) * BLOCK)
                cols = slice(c0 + grp * LANES, c0 + (grp + 1) * LANES)
                ya_scr[rows, cols] = (y[grp_rows] * g_ref[0, rows, cols].astype(F32)).astype(BF16)
        if (j + 1) % OUT_BLOCKS_A == 0:
            out_rows = slice((j + 1 - OUT_BLOCKS_A) * BLOCK, (j + 1) * BLOCK)
            y_out = (jnp.dot(ya_scr[out_rows, :], wo_ref[0:A_WIDTH, :], preferred_element_type=F32)
                     + jnp.dot(yb_ref[0, out_rows, :], wo_ref[A_WIDTH:, :], preferred_element_type=F32))
            o_ref[0, out_rows, :] = x_ref[0, out_rows, :] + mod_ref[0, 2:3, :] * y_out


def _attn_a_out(sink, proj, x, yb, w_out_bf, mod3, bounded_scores):
    bsz = proj.shape[0]
    tq = NB_A * BLOCK

    def kv_spec(col):
        return pl.BlockSpec((1, SEQ, LANES), lambda b, i: (b, 0, col // LANES))

    return pl.pallas_call(
        functools.partial(_attn_a_out_kernel, bounded_scores=bounded_scores),
        grid=(bsz, SEQ // tq),
        in_specs=([pl.BlockSpec(memory_space=pltpu.SMEM),
                   pl.BlockSpec((1, tq, A_WIDTH), lambda b, i: (b, i, DST["qa"] // A_WIDTH))]
                  + [kv_spec(DST["ka"] + n * LANES) for n in range(KA_VARIANTS)]
                  + [kv_spec(DST["va"] + n * LANES) for n in range(VA_VARIANTS)]
                  + [pl.BlockSpec((1, tq, A_WIDTH), lambda b, i: (b, i, DST["ga"] // A_WIDTH)),
                     pl.BlockSpec((1, tq, D_MODEL), lambda b, i: (b, i, 0)),
                     pl.BlockSpec((1, tq, B_WIDTH), lambda b, i: (b, i, 0)),
                     pl.BlockSpec((A_WIDTH + B_WIDTH, D_MODEL), lambda b, i: (0, 0)),
                     pl.BlockSpec((1, 3, D_MODEL), lambda b, i: (b, 0, 0))]),
        out_specs=pl.BlockSpec((1, tq, D_MODEL), lambda b, i: (b, i, 0)),
        out_shape=jax.ShapeDtypeStruct((bsz, SEQ, D_MODEL), F32),
        scratch_shapes=[pltpu.VMEM((tq, A_WIDTH), BF16)],
        compiler_params=pltpu.CompilerParams(
            dimension_semantics=("parallel", "parallel"), vmem_limit_bytes=VMEM_LIMIT),
        name="attn_a_out",
    )(sink, *([proj] * (2 + KA_VARIANTS + VA_VARIANTS)), x, yb, w_out_bf, mod3)


def _attn_b_kernel(lam_ref, q_ref, k_ref, vt_ref, g_ref, sg_ref, o_ref, *, bounded_scores):
    lq1, lk1, lq2, lk2 = (lam_ref[r:r + 1, :] for r in range(4))
    lam = (jnp.exp(jnp.sum(lq1 * lk1, axis=-1, keepdims=True))
           - jnp.exp(jnp.sum(lq2 * lk2, axis=-1, keepdims=True)) + LAMBDA_INIT)
    lo = lax.broadcasted_iota(jnp.int32, (TQ_B, LANES), 1) < HEAD_DIM

    units = [(qt, h) for qt in range(QT_B) for h in range(B_HEADS)]

    def scores_t(unit):
        qt, h = unit
        cols = slice(h * LANES, (h + 1) * LANES)
        qp = q_ref[0, qt * TQ_B:(qt + 1) * TQ_B, cols].astype(F32)
        q01 = jnp.concatenate([jnp.where(lo, qp, 0.0), jnp.where(lo, 0.0, qp)], axis=0).astype(BF16)
        return _nt_dot(k_ref[0, :, cols], q01)

    ahead = [scores_t(u) for u in units[:QK_AHEAD_B]]
    for n, (qt, h) in enumerate(units):
        cols = slice(h * LANES, (h + 1) * LANES)
        rows = slice(qt * TQ_B, (qt + 1) * TQ_B)
        s_t = ahead.pop(0)
        if n + QK_AHEAD_B < len(units):
            ahead.append(scores_t(units[n + QK_AHEAD_B]))
        if bounded_scores:
            e_t = jnp.exp2(s_t).astype(BF16)
        else:
            m = jnp.max(s_t, axis=0, keepdims=True)
            e_t = jnp.exp2(s_t - m).astype(BF16)
        ol = jnp.dot(vt_ref[0, h * VT_ROWS:(h + 1) * VT_ROWS, :], e_t,
                     preferred_element_type=F32)
        o0 = ol[:LANES, :TQ_B] / ol[LANES:LANES + 1, :TQ_B]
        o1 = ol[:LANES, TQ_B:] / ol[LANES:LANES + 1, TQ_B:]
        o = (o0 - lam * o1).T
        on = o * lax.rsqrt(jnp.mean(o * o, axis=-1, keepdims=True) + EPS)
        on = on * sg_ref[...] * (1.0 - LAMBDA_INIT)
        o_ref[0, rows, cols] = (on * g_ref[0, rows, cols].astype(F32)).astype(BF16)


def _attn_b(lam_params, proj, vt, subln_gain, bounded_scores):
    bsz = proj.shape[0]
    tq = QT_B * TQ_B
    return pl.pallas_call(
        functools.partial(_attn_b_kernel, bounded_scores=bounded_scores),
        grid=(bsz, SEQ // tq),
        in_specs=[pl.BlockSpec((4, HEAD_DIM), lambda b, i: (0, 0)),
                  pl.BlockSpec((1, tq, B_WIDTH), lambda b, i: (b, i, DST["qb"] // B_WIDTH)),
                  pl.BlockSpec((1, SEQ, B_WIDTH), lambda b, i: (b, 0, DST["kb"] // B_WIDTH)),
                  pl.BlockSpec((1, B_HEADS * VT_ROWS, SEQ), lambda b, i: (b, 0, 0)),
                  pl.BlockSpec((1, tq, B_WIDTH), lambda b, i: (b, i, DST["gb"] // B_WIDTH)),
                  pl.BlockSpec((1, LANES), lambda b, i: (0, 0))],
        out_specs=pl.BlockSpec((1, tq, B_WIDTH), lambda b, i: (b, i, 0)),
        out_shape=jax.ShapeDtypeStruct((bsz, SEQ, B_WIDTH), BF16),
        compiler_params=pltpu.CompilerParams(
            dimension_semantics=("parallel", "parallel"), vmem_limit_bytes=VMEM_LIMIT),
        name="attn_b",
    )(lam_params, proj, proj, vt, proj, subln_gain)


def _column_gains(q_norm_a, k_norm_a, q_norm_b, k_norm_b):
    g = jnp.ones((IN_WIDTH,), F32)
    g = g.at[SRC["qa"]:SRC["qa"] + 512].set(jnp.tile(q_norm_a.astype(F32), 8) * Q_PRESCALE)
    g = g.at[SRC["ka"]:SRC["ka"] + 128].set(jnp.tile(k_norm_a.astype(F32), 2))
    g = g.at[SRC["qb"]:SRC["qb"] + 512].set(jnp.tile(q_norm_b.astype(F32), 8) * Q_PRESCALE)
    g = g.at[SRC["kb"]:SRC["kb"] + 512].set(jnp.tile(k_norm_b.astype(F32), 8))
    return g.reshape(1, IN_WIDTH)


def kernel(x, c, positions, w_ada, b_ada, norm_gain, w_in, q_norm_a, k_norm_a, sink_a, q_norm_b,
           k_norm_b, lambda_q1, lambda_k1, lambda_q2, lambda_k2, subln_gain, w_out):
    assert w_ada.shape[0] == 1, "single-layer trunk"
    bsz = x.shape[0]
    inv_freq = 1.0 / (ROPE_THETA ** (jnp.arange(0, HEAD_DIM, 2, dtype=F32) / HEAD_DIM))
    ones_blk = jnp.asarray(np.kron(np.eye(4), np.ones((HEAD_DIM, HEAD_DIM))), dtype=BF16)
    colgain = _column_gains(q_norm_a[0], k_norm_a[0], q_norm_b[0], k_norm_b[0])
    lam_params = jnp.concatenate([lambda_q1, lambda_k1, lambda_q2, lambda_k2], axis=0).astype(F32)

    mod3 = _ada(c, w_ada[0], b_ada[0]).reshape(bsz, 3, D_MODEL)
    cos_t, sin_t = _rope_tables(positions, inv_freq)
    proj, vt = _inproj(x, cos_t, sin_t, mod3, norm_gain, w_in[0].astype(BF16), colgain, ones_blk)
    def score_bound(q_gain, k_gain):
        return (HEAD_DIM * Q_PRESCALE * jnp.max(jnp.abs(q_gain.astype(F32)))
                * jnp.max(jnp.abs(k_gain.astype(F32))))

    sink = sink_a[0].astype(F32)
    w_out_bf = w_out[0].astype(BF16)
    bound = jnp.maximum(jnp.maximum(score_bound(q_norm_b[0], k_norm_b[0]),
                                    score_bound(q_norm_a[0], k_norm_a[0])),
                        LOG2E * jnp.max(jnp.abs(sink)))

    def mixers(bounded_scores):
        yb = _attn_b(lam_params, proj, vt, subln_gain, bounded_scores)
        return _attn_a_out(sink, proj, x, yb, w_out_bf, mod3, bounded_scores)

    return lax.cond(bound <= SAFE_LOG2_SCORE, lambda: mixers(True), lambda: mixers(False))
```

```python
import functools
import math

import jax
import jax.numpy as jnp
import numpy as np
from jax import lax
from jax.experimental import pallas as pl
from jax.experimental.pallas import tpu as pltpu

F32 = jnp.float32
BF16 = jnp.bfloat16

D_MODEL = 1024
SEQ = 2048
HEAD_DIM = 64
LANES = 128
A_WIDTH = 512
A_GROUP = 4
MXU_TILE = 256
B_WIDTH = 512
B_HEADS = 4
WINDOW = 128
BLOCK = 128
SPAN = BLOCK + 2 * WINDOW
ROPE_THETA = 10000.0
EPS = 1e-6
LOG2E = math.log2(math.e)
Q_PRESCALE = LOG2E / math.sqrt(HEAD_DIM)
NEG = -1e30
SAFE_LOG2_SCORE = 64.0
LAMBDA_INIT = 0.8 - 0.6 * math.exp(-0.3 * 0)

SRC = dict(qa=0, ka=512, va=640, ga=768, qb=1280, kb=1792, vb=2304, gb=2816)
IN_WIDTH = 3328
DST = dict(qa=0, ga=512, qb=1024, kb=1536, gb=2048, ka=2560, va=3072)
KA_VARIANTS = 4
VA_VARIANTS = 4
PROJ_WIDTH = 3584
VT_ROWS = 144

VMEM_LIMIT = 48 * 1024 * 1024
TM_IN = 1024
NB_A = 8
TQ_B = 512
QT_B = 4
QK_AHEAD_B = 1
OUT_BLOCKS_A = 4


def _nt_dot(a, b):
    return lax.dot_general(a, b, (((1,), (1,)), ((), ())), preferred_element_type=F32)


def _ada_kernel(c_ref, w_ref, b_ref, o_ref):
    c = c_ref[...]
    sc = c * jax.nn.sigmoid(c)
    o_ref[...] = jnp.dot(sc.astype(BF16), w_ref[...].astype(BF16),
                         preferred_element_type=F32) + b_ref[...]


def _ada(c, w_ada, b_ada):
    bsz = c.shape[0]
    n = w_ada.shape[1]
    tn = D_MODEL
    return pl.pallas_call(
        _ada_kernel,
        grid=(n // tn,),
        in_specs=[pl.BlockSpec((bsz, D_MODEL), lambda j: (0, 0)),
                  pl.BlockSpec((D_MODEL, tn), lambda j: (0, j)),
                  pl.BlockSpec((1, tn), lambda j: (0, j))],
        out_specs=pl.BlockSpec((bsz, tn), lambda j: (0, j)),
        out_shape=jax.ShapeDtypeStruct((bsz, n), F32),
        compiler_params=pltpu.CompilerParams(vmem_limit_bytes=VMEM_LIMIT),
        name="ada",
    )(c, w_ada, b_ada.reshape(1, n))


_SECTION_KIND = dict(qa="qk", ka="ka", va="va", ga="gate", qb="qk", kb="qk", vb="v", gb="gate")
_SECTION_WIDTH = dict(qa=512, ka=128, va=128, ga=512, qb=512, kb=512, vb=512, gb=512)
MM_WIDTH = 1024
EPI_WIDTH = MXU_TILE


def _matmul_chunks():
    pieces = []
    for name in sorted(SRC, key=SRC.get):
        for off in range(0, _SECTION_WIDTH[name], EPI_WIDTH):
            w = min(EPI_WIDTH, _SECTION_WIDTH[name] - off)
            dst = off // LANES if name == "vb" else DST[name] + off
            pieces.append((_SECTION_KIND[name], SRC[name] + off, w, dst))
    chunks = []
    for kind, src, w, dst in pieces:
        if chunks and src + w - chunks[-1][0] <= MM_WIDTH:
            chunks[-1][2].append((kind, src - chunks[-1][0], w, dst))
            chunks[-1][1] = src + w - chunks[-1][0]
        else:
            chunks.append([src, w, [(kind, 0, w, dst)]])
    return tuple((s, w, tuple(p)) for s, w, p in chunks)


_MATMUL_CHUNKS = _matmul_chunks()


def _rope_kernel(pos_ref, invf_ref, cos_ref, sin_ref):
    ang = invf_ref[...] * pos_ref[0].astype(F32)
    cos = jnp.cos(ang)
    sin = jnp.sin(ang)
    cos4 = jnp.concatenate([cos, cos, cos, cos], axis=0)
    sin4 = jnp.concatenate([-sin, sin, -sin, sin], axis=0)
    cos_ref[0] = cos4.T
    sin_ref[0] = sin4.T


def _rope_tables(positions, inv_freq):
    bsz = positions.shape[0]
    nfreq = HEAD_DIM // 2
    return pl.pallas_call(
        _rope_kernel,
        grid=(bsz,),
        in_specs=[pl.BlockSpec((1, 1, SEQ), lambda b: (b, 0, 0)),
                  pl.BlockSpec((nfreq, 1), lambda b: (0, 0))],
        out_specs=[pl.BlockSpec((1, SEQ, LANES), lambda b: (b, 0, 0)),
                   pl.BlockSpec((1, SEQ, LANES), lambda b: (b, 0, 0))],
        out_shape=[jax.ShapeDtypeStruct((bsz, SEQ, LANES), F32)] * 2,
        compiler_params=pltpu.CompilerParams(
            dimension_semantics=("parallel",), vmem_limit_bytes=VMEM_LIMIT),
        name="rope",
    )(positions.reshape(bsz, 1, SEQ), inv_freq.reshape(nfreq, 1))


def _inproj_kernel(x_ref, cos_ref, sin_ref, mod_ref, ng_ref, w_ref, cg_ref, ones_ref,
                   o_ref, vt_ref, h_scr):
    x = x_ref[0]
    ms = jnp.mean(x * x, axis=-1, keepdims=True)
    xn = x * lax.rsqrt(ms + EPS)
    shift = mod_ref[0, 0:1, :]
    scale = mod_ref[0, 1:2, :]
    h = (xn * ng_ref[...]) * (1.0 + scale) + shift
    h_scr[...] = h.astype(BF16)

    cos = cos_ref[0]
    sin = sin_ref[0]
    lane = lax.broadcasted_iota(jnp.int32, cos.shape, 1)
    first_half = (lane & (HEAD_DIM - 1)) < HEAD_DIM // 2
    low_head = lane < HEAD_DIM

    for src0, width, pieces in _MATMUL_CHUNKS:
        pm = jnp.dot(h_scr[...], w_ref[:, src0:src0 + width], preferred_element_type=F32)
        for kind, off, w, dst in pieces:
            src = src0 + off
            p = pm[:, off:off + w]
            if kind == "va":
                p_sw = pltpu.roll(p, HEAD_DIM, 1)
                variants = (jnp.where(low_head, p, 0.0), jnp.where(low_head, 0.0, p_sw),
                            jnp.where(low_head, p_sw, 0.0), jnp.where(low_head, 0.0, p))
                for n, vvar in enumerate(variants):
                    o_ref[0, :, dst + n * LANES:dst + (n + 1) * LANES] = vvar.astype(BF16)
            elif kind == "v":
                p_t = p.T
                for n in range(w // LANES):
                    r0 = (dst + n) * VT_ROWS
                    vt_ref[0, r0:r0 + LANES, :] = p_t[n * LANES:(n + 1) * LANES, :].astype(BF16)
                    vt_ref[0, r0 + LANES:r0 + VT_ROWS, :] = jnp.ones((VT_ROWS - LANES, p_t.shape[1]), BF16)
            elif kind == "gate":
                o_ref[0, :, dst:dst + w] = (p * jax.nn.sigmoid(p)).astype(BF16)
            else:
                ss = jnp.dot((p * p).astype(BF16), ones_ref[0:w, 0:w], preferred_element_type=F32)
                pn = p * lax.rsqrt(ss * (1.0 / HEAD_DIM) + EPS) * cg_ref[:, src:src + w]
                for j in range(w // LANES):
                    t = pn[:, j * LANES:(j + 1) * LANES]
                    rot = jnp.where(first_half, pltpu.roll(t, LANES - HEAD_DIM // 2, 1),
                                    pltpu.roll(t, HEAD_DIM // 2, 1))
                    r = t * cos + rot * sin
                    if kind == "ka":
                        r_sw = pltpu.roll(r, HEAD_DIM, 1)
                        variants = (jnp.where(low_head, r, 0.0), jnp.where(low_head, 0.0, r_sw),
                                    jnp.where(low_head, r_sw, 0.0), jnp.where(low_head, 0.0, r))
                        for n, kvar in enumerate(variants):
                            o_ref[0, :, dst + n * LANES:dst + (n + 1) * LANES] = kvar.astype(BF16)
                    else:
                        o_ref[0, :, dst + j * LANES:dst + (j + 1) * LANES] = r.astype(BF16)


def _inproj(x, cos_t, sin_t, mod3, norm_gain, w_in_bf, colgain, ones_blk):
    bsz = x.shape[0]
    return pl.pallas_call(
        _inproj_kernel,
        grid=(bsz, SEQ // TM_IN),
        in_specs=[pl.BlockSpec((1, TM_IN, D_MODEL), lambda b, i: (b, i, 0)),
                  pl.BlockSpec((1, TM_IN, LANES), lambda b, i: (b, i, 0)),
                  pl.BlockSpec((1, TM_IN, LANES), lambda b, i: (b, i, 0)),
                  pl.BlockSpec((1, 3, D_MODEL), lambda b, i: (b, 0, 0)),
                  pl.BlockSpec((1, D_MODEL), lambda b, i: (0, 0)),
                  pl.BlockSpec((D_MODEL, IN_WIDTH), lambda b, i: (0, 0)),
                  pl.BlockSpec((1, IN_WIDTH), lambda b, i: (0, 0)),
                  pl.BlockSpec((MXU_TILE, MXU_TILE), lambda b, i: (0, 0))],
        out_specs=[pl.BlockSpec((1, TM_IN, PROJ_WIDTH), lambda b, i: (b, i, 0)),
                   pl.BlockSpec((1, B_HEADS * VT_ROWS, TM_IN), lambda b, i: (b, 0, i))],
        out_shape=[jax.ShapeDtypeStruct((bsz, SEQ, PROJ_WIDTH), BF16),
                   jax.ShapeDtypeStruct((bsz, B_HEADS * VT_ROWS, SEQ), BF16)],
        scratch_shapes=[pltpu.VMEM((TM_IN, D_MODEL), BF16)],
        compiler_params=pltpu.CompilerParams(
            dimension_semantics=("parallel", "parallel"), vmem_limit_bytes=VMEM_LIMIT),
        name="inproj",
    )(x, cos_t, sin_t, mod3, norm_gain, w_in_bf, colgain, ones_blk)


def _attn_a_out_kernel(sink_ref, q_ref, ke0_ref, ko0_ref, ke1_ref, ko1_ref, ve0_ref, vo0_ref, ve1_ref,
                       vo1_ref, g_ref, x_ref, yb_ref, wo_ref, mod_ref, o_ref, ya_scr, *, bounded_scores):
    i = pl.program_id(1)
    k_refs = ((ke0_ref, ko0_ref), (ke1_ref, ko1_ref))
    v_refs = ((ve0_ref, vo0_ref), (ve1_ref, vo1_ref))
    row_minus_col = ((lax.broadcasted_iota(jnp.int32, (2 * BLOCK, SPAN), 0) & (BLOCK - 1))
                     - lax.broadcasted_iota(jnp.int32, (2 * BLOCK, SPAN), 1))
    lo_out = lax.broadcasted_iota(jnp.int32, (2 * BLOCK, LANES), 1) < HEAD_DIM
    first_group = lax.broadcasted_iota(jnp.int32, (2 * BLOCK, 1), 0) < BLOCK
    sinks = [[jnp.where(first_group, sink_ref[A_GROUP * kv + parity], sink_ref[A_GROUP * kv + 2 + parity])
              * LOG2E for parity in range(2)] for kv in range(2)]

    def both_parities(refs, start):
        return jnp.concatenate([r[0, pl.ds(start, SPAN), :] for r in refs], axis=0)

    def scores(j):
        blk = i * NB_A + j
        start = pl.multiple_of(jnp.clip(blk * BLOCK - WINDOW, 0, SEQ - SPAN), BLOCK)
        valid = jnp.abs(row_minus_col + (blk * BLOCK - start)) <= WINDOW
        valid = jnp.concatenate([valid, valid], axis=1)
        rows = slice(j * BLOCK, (j + 1) * BLOCK)
        out = []
        for kv in range(2):
            c0 = 2 * kv * LANES
            q2 = jnp.concatenate([q_ref[0, rows, c0:c0 + LANES], q_ref[0, rows, c0 + LANES:c0 + 2 * LANES]],
                                 axis=0)
            s = _nt_dot(q2, both_parities(k_refs[kv], start))
            out.append(jnp.where(valid, s, NEG))
        return start, out

    nxt = scores(0)
    for j in range(NB_A):
        start, s_list = nxt
        if j + 1 < NB_A:
            nxt = scores(j + 1)
        rows = slice(j * BLOCK, (j + 1) * BLOCK)
        for kv in range(2):
            c0 = 2 * kv * LANES
            ps, denoms = [], []
            for parity in range(2):
                s = s_list[kv][:, parity * SPAN:(parity + 1) * SPAN]
                sink = sinks[kv][parity]
                if bounded_scores:
                    p = jnp.exp2(s)
                    denom = jnp.sum(p, axis=-1, keepdims=True) + jnp.exp2(sink)
                else:
                    m = jnp.maximum(jnp.max(s, axis=-1, keepdims=True), sink)
                    p = jnp.exp2(s - m)
                    denom = jnp.sum(p, axis=-1, keepdims=True) + jnp.exp2(sink - m)
                ps.append(p.astype(BF16))
                denoms.append(denom)
            o = jnp.dot(jnp.concatenate(ps, axis=1), both_parities(v_refs[kv], start),
                        preferred_element_type=F32)
            y = o / jnp.where(lo_out, denoms[0], denoms[1])
            for grp in range(2):
                grp_rows = slice(grp * BLOCK, (grp + 1) * BLOCK)
                cols = slice(c0 + grp * LANES, c0 + (grp + 1) * LANES)
                ya_scr[rows, cols] = (y[grp_rows] * g_ref[0, rows, cols].astype(F32)).astype(BF16)
        if (j + 1) % OUT_BLOCKS_A == 0:
            out_rows = slice((j + 1 - OUT_BLOCKS_A) * BLOCK, (j + 1) * BLOCK)
            y_out = (jnp.dot(ya_scr[out_rows, :], wo_ref[0:A_WIDTH, :], preferred_element_type=F32)
                     + jnp.dot(yb_ref[0, out_rows, :], wo_ref[A_WIDTH:, :], preferred_element_type=F32))
            o_ref[0, out_rows, :] = x_ref[0, out_rows, :] + mod_ref[0, 2:3, :] * y_out


def _attn_a_out(sink, proj, x, yb, w_out_bf, mod3, bounded_scores):
    bsz = proj.shape[0]
    tq = NB_A * BLOCK

    def kv_spec(col):
        return pl.BlockSpec((1, SEQ, LANES), lambda b, i: (b, 0, col // LANES))

    return pl.pallas_call(
        functools.partial(_attn_a_out_kernel, bounded_scores=bounded_scores),
        grid=(bsz, SEQ // tq),
        in_specs=([pl.BlockSpec(memory_space=pltpu.SMEM),
                   pl.BlockSpec((1, tq, A_WIDTH), lambda b, i: (b, i, DST["qa"] // A_WIDTH))]
                  + [kv_spec(DST["ka"] + n * LANES) for n in range(KA_VARIANTS)]
                  + [kv_spec(DST["va"] + n * LANES) for n in range(VA_VARIANTS)]
                  + [pl.BlockSpec((1, tq, A_WIDTH), lambda b, i: (b, i, DST["ga"] // A_WIDTH)),
                     pl.BlockSpec((1, tq, D_MODEL), lambda b, i: (b, i, 0)),
                     pl.BlockSpec((1, tq, B_WIDTH), lambda b, i: (b, i, 0)),
                     pl.BlockSpec((A_WIDTH + B_WIDTH, D_MODEL), lambda b, i: (0, 0)),
                     pl.BlockSpec((1, 3, D_MODEL), lambda b, i: (b, 0, 0))]),
        out_specs=pl.BlockSpec((1, tq, D_MODEL), lambda b, i: (b, i, 0)),
        out_shape=jax.ShapeDtypeStruct((bsz, SEQ, D_MODEL), F32),
        scratch_shapes=[pltpu.VMEM((tq, A_WIDTH), BF16)],
        compiler_params=pltpu.CompilerParams(
            dimension_semantics=("parallel", "parallel"), vmem_limit_bytes=VMEM_LIMIT),
        name="attn_a_out",
    )(sink, *([proj] * (2 + KA_VARIANTS + VA_VARIANTS)), x, yb, w_out_bf, mod3)


def _attn_b_kernel(lam_ref, q_ref, k_ref, vt_ref, g_ref, sg_ref, o_ref, *, bounded_scores):
    lq1, lk1, lq2, lk2 = (lam_ref[r:r + 1, :] for r in range(4))
    lam = (jnp.exp(jnp.sum(lq1 * lk1, axis=-1, keepdims=True))
           - jnp.exp(jnp.sum(lq2 * lk2, axis=-1, keepdims=True)) + LAMBDA_INIT)
    lo = lax.broadcasted_iota(jnp.int32, (TQ_B, LANES), 1) < HEAD_DIM

    units = [(qt, h) for qt in range(QT_B) for h in range(B_HEADS)]

    def scores_t(unit):
        qt, h = unit
        cols = slice(h * LANES, (h + 1) * LANES)
        qp = q_ref[0, qt * TQ_B:(qt + 1) * TQ_B, cols].astype(F32)
        q01 = jnp.concatenate([jnp.where(lo, qp, 0.0), jnp.where(lo, 0.0, qp)], axis=0).astype(BF16)
        return _nt_dot(k_ref[0, :, cols], q01)

    ahead = [scores_t(u) for u in units[:QK_AHEAD_B]]
    for n, (qt, h) in enumerate(units):
        cols = slice(h * LANES, (h + 1) * LANES)
        rows = slice(qt * TQ_B, (qt + 1) * TQ_B)
        s_t = ahead.pop(0)
        if n + QK_AHEAD_B < len(units):
            ahead.append(scores_t(units[n + QK_AHEAD_B]))
        if bounded_scores:
            e_t = jnp.exp2(s_t).astype(BF16)
        else:
            m = jnp.max(s_t, axis=0, keepdims=True)
            e_t = jnp.exp2(s_t - m).astype(BF16)
        ol = jnp.dot(vt_ref[0, h * VT_ROWS:(h + 1) * VT_ROWS, :], e_t,
                     preferred_element_type=F32)
        o0 = ol[:LANES, :TQ_B] / ol[LANES:LANES + 1, :TQ_B]
        o1 = ol[:LANES, TQ_B:] / ol[LANES:LANES + 1, TQ_B:]
        o = (o0 - lam * o1).T
        on = o * lax.rsqrt(jnp.mean(o * o, axis=-1, keepdims=True) + EPS)
        on = on * sg_ref[...] * (1.0 - LAMBDA_INIT)
        o_ref[0, rows, cols] = (on * g_ref[0, rows, cols].astype(F32)).astype(BF16)


def _attn_b(lam_params, proj, vt, subln_gain, bounded_scores):
    bsz = proj.shape[0]
    tq = QT_B * TQ_B
    return pl.pallas_call(
        functools.partial(_attn_b_kernel, bounded_scores=bounded_scores),
        grid=(bsz, SEQ // tq),
        in_specs=[pl.BlockSpec((4, HEAD_DIM), lambda b, i: (0, 0)),
                  pl.BlockSpec((1, tq, B_WIDTH), lambda b, i: (b, i, DST["qb"] // B_WIDTH)),
                  pl.BlockSpec((1, SEQ, B_WIDTH), lambda b, i: (b, 0, DST["kb"] // B_WIDTH)),
                  pl.BlockSpec((1, B_HEADS * VT_ROWS, SEQ), lambda b, i: (b, 0, 0)),
                  pl.BlockSpec((1, tq, B_WIDTH), lambda b, i: (b, i, DST["gb"] // B_WIDTH)),
                  pl.BlockSpec((1, LANES), lambda b, i: (0, 0))],
        out_specs=pl.BlockSpec((1, tq, B_WIDTH), lambda b, i: (b, i, 0)),
        out_shape=jax.ShapeDtypeStruct((bsz, SEQ, B_WIDTH), BF16),
        compiler_params=pltpu.CompilerParams(
            dimension_semantics=("parallel", "parallel"), vmem_limit_bytes=VMEM_LIMIT),
        name="attn_b",
    )(lam_params, proj, proj, vt, proj, subln_gain)


def _column_gains(q_norm_a, k_norm_a, q_norm_b, k_norm_b):
    g = jnp.ones((IN_WIDTH,), F32)
    for name, gain, scale in (("qa", q_norm_a, Q_PRESCALE), ("ka", k_norm_a, 1.0),
                              ("qb", q_norm_b, Q_PRESCALE), ("kb", k_norm_b, 1.0)):
        width = _SECTION_WIDTH[name]
        g = g.at[SRC[name]:SRC[name] + width].set(jnp.tile(gain.astype(F32), width // HEAD_DIM) * scale)
    return g.reshape(1, IN_WIDTH)


def kernel(x, c, positions, w_ada, b_ada, norm_gain, w_in, q_norm_a, k_norm_a, sink_a, q_norm_b,
           k_norm_b, lambda_q1, lambda_k1, lambda_q2, lambda_k2, subln_gain, w_out):
    assert w_ada.shape[0] == 1, "single-layer trunk"
    bsz = x.shape[0]
    inv_freq = 1.0 / (ROPE_THETA ** (jnp.arange(0, HEAD_DIM, 2, dtype=F32) / HEAD_DIM))
    ones_blk = jnp.asarray(np.kron(np.eye(MXU_TILE // HEAD_DIM), np.ones((HEAD_DIM, HEAD_DIM))), dtype=BF16)
    colgain = _column_gains(q_norm_a[0], k_norm_a[0], q_norm_b[0], k_norm_b[0])
    lam_params = jnp.concatenate([lambda_q1, lambda_k1, lambda_q2, lambda_k2], axis=0).astype(F32)

    mod3 = _ada(c, w_ada[0], b_ada[0]).reshape(bsz, 3, D_MODEL)
    cos_t, sin_t = _rope_tables(positions, inv_freq)
    proj, vt = _inproj(x, cos_t, sin_t, mod3, norm_gain, w_in[0].astype(BF16), colgain, ones_blk)
    def score_bound(q_gain, k_gain):
        return (HEAD_DIM * Q_PRESCALE * jnp.max(jnp.abs(q_gain.astype(F32)))
                * jnp.max(jnp.abs(k_gain.astype(F32))))

    sink = sink_a[0].astype(F32)
    w_out_bf = w_out[0].astype(BF16)
    bound = jnp.maximum(jnp.maximum(score_bound(q_norm_b[0], k_norm_b[0]),
                                    score_bound(q_norm_a[0], k_norm_a[0])),
                        LOG2E * jnp.max(jnp.abs(sink)))

    def mixers(bounded_scores):
        yb = _attn_b(lam_params, proj, vt, subln_gain, bounded_scores)
        return _attn_a_out(sink, proj, x, yb, w_out_bf, mod3, bounded_scores)

    return lax.cond(bound <= SAFE_LOG2_SCORE, lambda: mixers(True), lambda: mixers(False))
```

```python
import functools
import math

import jax
import jax.numpy as jnp
import numpy as np
from jax import lax
from jax.experimental import pallas as pl
from jax.experimental.pallas import tpu as pltpu

F32 = jnp.float32
BF16 = jnp.bfloat16

D_MODEL = 1024
SEQ = 2048
HEAD_DIM = 64
LANES = 128
A_WIDTH = 512
A_GROUP = 4
MXU_TILE = 256
B_WIDTH = 512
B_HEADS = 4
WINDOW = 128
BLOCK = 128
SPAN = BLOCK + 2 * WINDOW
ROPE_THETA = 10000.0
EPS = 1e-6
LOG2E = math.log2(math.e)
Q_PRESCALE = LOG2E / math.sqrt(HEAD_DIM)
NEG = -1e30
SAFE_LOG2_SCORE = 64.0
LAMBDA_INIT = 0.8 - 0.6 * math.exp(-0.3 * 0)

SRC = dict(qa=0, ka=512, va=640, ga=768, qb=1280, kb=1792, vb=2304, gb=2816)
IN_WIDTH = 3328
DST = dict(qa=0, ga=512, qb=1024, kb=1536, gb=2048, ka=2560, va=3072)
KA_VARIANTS = 4
VA_VARIANTS = 4
PROJ_WIDTH = 3584
VT_ROWS = 144

VMEM_LIMIT = 48 * 1024 * 1024
TM_IN = 1024
NB_A = 8
TQ_B = 512
QT_B = 4
QK_AHEAD_B = 1
OUT_GROUP_ENDS_A = (4, 6, 8)


def _nt_dot(a, b):
    return lax.dot_general(a, b, (((1,), (1,)), ((), ())), preferred_element_type=F32)


def _ada_kernel(c_ref, w_ref, b_ref, o_ref):
    c = c_ref[...]
    sc = c * jax.nn.sigmoid(c)
    o_ref[...] = jnp.dot(sc.astype(BF16), w_ref[...].astype(BF16),
                         preferred_element_type=F32) + b_ref[...]


def _ada(c, w_ada, b_ada):
    bsz = c.shape[0]
    n = w_ada.shape[1]
    tn = D_MODEL
    return pl.pallas_call(
        _ada_kernel,
        grid=(n // tn,),
        in_specs=[pl.BlockSpec((bsz, D_MODEL), lambda j: (0, 0)),
                  pl.BlockSpec((D_MODEL, tn), lambda j: (0, j)),
                  pl.BlockSpec((1, tn), lambda j: (0, j))],
        out_specs=pl.BlockSpec((bsz, tn), lambda j: (0, j)),
        out_shape=jax.ShapeDtypeStruct((bsz, n), F32),
        compiler_params=pltpu.CompilerParams(vmem_limit_bytes=VMEM_LIMIT),
        name="ada",
    )(c, w_ada, b_ada.reshape(1, n))


_SECTION_KIND = dict(qa="qk", ka="ka", va="va", ga="gate", qb="qk", kb="qk", vb="v", gb="gate")
_SECTION_WIDTH = dict(qa=512, ka=128, va=128, ga=512, qb=512, kb=512, vb=512, gb=512)
MM_WIDTH = 1024
EPI_WIDTH = MXU_TILE


def _matmul_chunks():
    pieces = []
    for name in sorted(SRC, key=SRC.get):
        for off in range(0, _SECTION_WIDTH[name], EPI_WIDTH):
            w = min(EPI_WIDTH, _SECTION_WIDTH[name] - off)
            dst = off // LANES if name == "vb" else DST[name] + off
            pieces.append((_SECTION_KIND[name], SRC[name] + off, w, dst))
    chunks = []
    for kind, src, w, dst in pieces:
        if chunks and src + w - chunks[-1][0] <= MM_WIDTH:
            chunks[-1][2].append((kind, src - chunks[-1][0], w, dst))
            chunks[-1][1] = src + w - chunks[-1][0]
        else:
            chunks.append([src, w, [(kind, 0, w, dst)]])
    return tuple((s, w, tuple(p)) for s, w, p in chunks)


_MATMUL_CHUNKS = _matmul_chunks()


def _rope_kernel(pos_ref, invf_ref, cos_ref, sin_ref):
    ang = invf_ref[...] * pos_ref[0].astype(F32)
    cos = jnp.cos(ang)
    sin = jnp.sin(ang)
    cos4 = jnp.concatenate([cos, cos, cos, cos], axis=0)
    sin4 = jnp.concatenate([-sin, sin, -sin, sin], axis=0)
    cos_ref[0] = cos4.T
    sin_ref[0] = sin4.T


def _rope_tables(positions, inv_freq):
    bsz = positions.shape[0]
    nfreq = HEAD_DIM // 2
    return pl.pallas_call(
        _rope_kernel,
        grid=(bsz,),
        in_specs=[pl.BlockSpec((1, 1, SEQ), lambda b: (b, 0, 0)),
                  pl.BlockSpec((nfreq, 1), lambda b: (0, 0))],
        out_specs=[pl.BlockSpec((1, SEQ, LANES), lambda b: (b, 0, 0)),
                   pl.BlockSpec((1, SEQ, LANES), lambda b: (b, 0, 0))],
        out_shape=[jax.ShapeDtypeStruct((bsz, SEQ, LANES), F32)] * 2,
        compiler_params=pltpu.CompilerParams(
            dimension_semantics=("parallel",), vmem_limit_bytes=VMEM_LIMIT),
        name="rope",
    )(positions.reshape(bsz, 1, SEQ), inv_freq.reshape(nfreq, 1))


def _inproj_kernel(x_ref, cos_ref, sin_ref, mod_ref, ng_ref, w_ref, cg_ref, ones_ref,
                   o_ref, vt_ref, h_scr):
    x = x_ref[0]
    ms = jnp.mean(x * x, axis=-1, keepdims=True)
    xn = x * lax.rsqrt(ms + EPS)
    shift = mod_ref[0, 0:1, :]
    scale = mod_ref[0, 1:2, :]
    h = (xn * ng_ref[...]) * (1.0 + scale) + shift
    h_scr[...] = h.astype(BF16)

    cos = cos_ref[0]
    sin = sin_ref[0]
    lane = lax.broadcasted_iota(jnp.int32, cos.shape, 1)
    first_half = (lane & (HEAD_DIM - 1)) < HEAD_DIM // 2
    low_head = lane < HEAD_DIM

    for src0, width, pieces in _MATMUL_CHUNKS:
        pm = jnp.dot(h_scr[...], w_ref[:, src0:src0 + width], preferred_element_type=F32)
        for kind, off, w, dst in pieces:
            src = src0 + off
            p = pm[:, off:off + w]
            if kind == "va":
                p_sw = pltpu.roll(p, HEAD_DIM, 1)
                variants = (jnp.where(low_head, p, 0.0), jnp.where(low_head, 0.0, p_sw),
                            jnp.where(low_head, p_sw, 0.0), jnp.where(low_head, 0.0, p))
                for n, vvar in enumerate(variants):
                    o_ref[0, :, dst + n * LANES:dst + (n + 1) * LANES] = vvar.astype(BF16)
            elif kind == "v":
                p_t = p.T
                for n in range(w // LANES):
                    r0 = (dst + n) * VT_ROWS
                    vt_ref[0, r0:r0 + LANES, :] = p_t[n * LANES:(n + 1) * LANES, :].astype(BF16)
                    vt_ref[0, r0 + LANES:r0 + VT_ROWS, :] = jnp.ones((VT_ROWS - LANES, p_t.shape[1]), BF16)
            elif kind == "gate":
                o_ref[0, :, dst:dst + w] = (p * jax.nn.sigmoid(p)).astype(BF16)
            else:
                ss = jnp.dot((p * p).astype(BF16), ones_ref[0:w, 0:w], preferred_element_type=F32)
                pn = p * lax.rsqrt(ss * (1.0 / HEAD_DIM) + EPS) * cg_ref[:, src:src + w]
                for j in range(w // LANES):
                    t = pn[:, j * LANES:(j + 1) * LANES]
                    rot = jnp.where(first_half, pltpu.roll(t, LANES - HEAD_DIM // 2, 1),
                                    pltpu.roll(t, HEAD_DIM // 2, 1))
                    r = t * cos + rot * sin
                    if kind == "ka":
                        r_sw = pltpu.roll(r, HEAD_DIM, 1)
                        variants = (jnp.where(low_head, r, 0.0), jnp.where(low_head, 0.0, r_sw),
                                    jnp.where(low_head, r_sw, 0.0), jnp.where(low_head, 0.0, r))
                        for n, kvar in enumerate(variants):
                            o_ref[0, :, dst + n * LANES:dst + (n + 1) * LANES] = kvar.astype(BF16)
                    else:
                        o_ref[0, :, dst + j * LANES:dst + (j + 1) * LANES] = r.astype(BF16)


def _inproj(x, cos_t, sin_t, mod3, norm_gain, w_in_bf, colgain, ones_blk):
    bsz = x.shape[0]
    return pl.pallas_call(
        _inproj_kernel,
        grid=(bsz, SEQ // TM_IN),
        in_specs=[pl.BlockSpec((1, TM_IN, D_MODEL), lambda b, i: (b, i, 0)),
                  pl.BlockSpec((1, TM_IN, LANES), lambda b, i: (b, i, 0)),
                  pl.BlockSpec((1, TM_IN, LANES), lambda b, i: (b, i, 0)),
                  pl.BlockSpec((1, 3, D_MODEL), lambda b, i: (b, 0, 0)),
                  pl.BlockSpec((1, D_MODEL), lambda b, i: (0, 0)),
                  pl.BlockSpec((D_MODEL, IN_WIDTH), lambda b, i: (0, 0)),
                  pl.BlockSpec((1, IN_WIDTH), lambda b, i: (0, 0)),
                  pl.BlockSpec((MXU_TILE, MXU_TILE), lambda b, i: (0, 0))],
        out_specs=[pl.BlockSpec((1, TM_IN, PROJ_WIDTH), lambda b, i: (b, i, 0)),
                   pl.BlockSpec((1, B_HEADS * VT_ROWS, TM_IN), lambda b, i: (b, 0, i))],
        out_shape=[jax.ShapeDtypeStruct((bsz, SEQ, PROJ_WIDTH), BF16),
                   jax.ShapeDtypeStruct((bsz, B_HEADS * VT_ROWS, SEQ), BF16)],
        scratch_shapes=[pltpu.VMEM((TM_IN, D_MODEL), BF16)],
        compiler_params=pltpu.CompilerParams(
            dimension_semantics=("parallel", "parallel"), vmem_limit_bytes=VMEM_LIMIT),
        name="inproj",
    )(x, cos_t, sin_t, mod3, norm_gain, w_in_bf, colgain, ones_blk)


def _attn_a_out_kernel(sink_ref, q_ref, ke0_ref, ko0_ref, ke1_ref, ko1_ref, ve0_ref, vo0_ref, ve1_ref,
                       vo1_ref, g_ref, x_ref, yb_ref, wo_ref, mod_ref, o_ref, ya_scr, *, bounded_scores):
    i = pl.program_id(1)
    k_refs = ((ke0_ref, ko0_ref), (ke1_ref, ko1_ref))
    v_refs = ((ve0_ref, vo0_ref), (ve1_ref, vo1_ref))
    row_minus_col = ((lax.broadcasted_iota(jnp.int32, (2 * BLOCK, SPAN), 0) & (BLOCK - 1))
                     - lax.broadcasted_iota(jnp.int32, (2 * BLOCK, SPAN), 1))
    lo_out = lax.broadcasted_iota(jnp.int32, (2 * BLOCK, LANES), 1) < HEAD_DIM
    first_group = lax.broadcasted_iota(jnp.int32, (2 * BLOCK, 1), 0) < BLOCK
    sinks = [[jnp.where(first_group, sink_ref[A_GROUP * kv + parity], sink_ref[A_GROUP * kv + 2 + parity])
              * LOG2E for parity in range(2)] for kv in range(2)]

    def both_parities(refs, start):
        return jnp.concatenate([r[0, pl.ds(start, SPAN), :] for r in refs], axis=0)

    def scores(j):
        blk = i * NB_A + j
        start = pl.multiple_of(jnp.clip(blk * BLOCK - WINDOW, 0, SEQ - SPAN), BLOCK)
        valid = jnp.abs(row_minus_col + (blk * BLOCK - start)) <= WINDOW
        valid = jnp.concatenate([valid, valid], axis=1)
        rows = slice(j * BLOCK, (j + 1) * BLOCK)
        out = []
        for kv in range(2):
            c0 = 2 * kv * LANES
            q2 = jnp.concatenate([q_ref[0, rows, c0:c0 + LANES], q_ref[0, rows, c0 + LANES:c0 + 2 * LANES]],
                                 axis=0)
            s = _nt_dot(q2, both_parities(k_refs[kv], start))
            out.append(jnp.where(valid, s, NEG))
        return start, out

    nxt = scores(0)
    for j in range(NB_A):
        start, s_list = nxt
        if j + 1 < NB_A:
            nxt = scores(j + 1)
        rows = slice(j * BLOCK, (j + 1) * BLOCK)
        for kv in range(2):
            c0 = 2 * kv * LANES
            ps, denoms = [], []
            for parity in range(2):
                s = s_list[kv][:, parity * SPAN:(parity + 1) * SPAN]
                sink = sinks[kv][parity]
                if bounded_scores:
                    p = jnp.exp2(s)
                    denom = jnp.sum(p, axis=-1, keepdims=True) + jnp.exp2(sink)
                else:
                    m = jnp.maximum(jnp.max(s, axis=-1, keepdims=True), sink)
                    p = jnp.exp2(s - m)
                    denom = jnp.sum(p, axis=-1, keepdims=True) + jnp.exp2(sink - m)
                ps.append(p.astype(BF16))
                denoms.append(denom)
            o = jnp.dot(jnp.concatenate(ps, axis=1), both_parities(v_refs[kv], start),
                        preferred_element_type=F32)
            y = o / jnp.where(lo_out, denoms[0], denoms[1])
            for grp in range(2):
                grp_rows = slice(grp * BLOCK, (grp + 1) * BLOCK)
                cols = slice(c0 + grp * LANES, c0 + (grp + 1) * LANES)
                ya_scr[rows, cols] = (y[grp_rows] * g_ref[0, rows, cols].astype(F32)).astype(BF16)
        if j + 1 in OUT_GROUP_ENDS_A:
            group_start = max([0] + [e for e in OUT_GROUP_ENDS_A if e <= j])
            out_rows = slice(group_start * BLOCK, (j + 1) * BLOCK)
            y_out = (jnp.dot(ya_scr[out_rows, :], wo_ref[0:A_WIDTH, :], preferred_element_type=F32)
                     + jnp.dot(yb_ref[0, out_rows, :], wo_ref[A_WIDTH:, :], preferred_element_type=F32))
            o_ref[0, out_rows, :] = x_ref[0, out_rows, :] + mod_ref[0, 2:3, :] * y_out


def _attn_a_out(sink, proj, x, yb, w_out_bf, mod3, bounded_scores):
    bsz = proj.shape[0]
    tq = NB_A * BLOCK

    def kv_spec(col):
        return pl.BlockSpec((1, SEQ, LANES), lambda b, i: (b, 0, col // LANES))

    return pl.pallas_call(
        functools.partial(_attn_a_out_kernel, bounded_scores=bounded_scores),
        grid=(bsz, SEQ // tq),
        in_specs=([pl.BlockSpec(memory_space=pltpu.SMEM),
                   pl.BlockSpec((1, tq, A_WIDTH), lambda b, i: (b, i, DST["qa"] // A_WIDTH))]
                  + [kv_spec(DST["ka"] + n * LANES) for n in range(KA_VARIANTS)]
                  + [kv_spec(DST["va"] + n * LANES) for n in range(VA_VARIANTS)]
                  + [pl.BlockSpec((1, tq, A_WIDTH), lambda b, i: (b, i, DST["ga"] // A_WIDTH)),
                     pl.BlockSpec((1, tq, D_MODEL), lambda b, i: (b, i, 0)),
                     pl.BlockSpec((1, tq, B_WIDTH), lambda b, i: (b, i, 0)),
                     pl.BlockSpec((A_WIDTH + B_WIDTH, D_MODEL), lambda b, i: (0, 0)),
                     pl.BlockSpec((1, 3, D_MODEL), lambda b, i: (b, 0, 0))]),
        out_specs=pl.BlockSpec((1, tq, D_MODEL), lambda b, i: (b, i, 0)),
        out_shape=jax.ShapeDtypeStruct((bsz, SEQ, D_MODEL), F32),
        scratch_shapes=[pltpu.VMEM((tq, A_WIDTH), BF16)],
        compiler_params=pltpu.CompilerParams(
            dimension_semantics=("parallel", "parallel"), vmem_limit_bytes=VMEM_LIMIT),
        name="attn_a_out",
    )(sink, *([proj] * (2 + KA_VARIANTS + VA_VARIANTS)), x, yb, w_out_bf, mod3)


def _attn_b_kernel(lam_ref, q_ref, k_ref, vt_ref, g_ref, sg_ref, o_ref, *, bounded_scores):
    lq1, lk1, lq2, lk2 = (lam_ref[r:r + 1, :] for r in range(4))
    lam = (jnp.exp(jnp.sum(lq1 * lk1, axis=-1, keepdims=True))
           - jnp.exp(jnp.sum(lq2 * lk2, axis=-1, keepdims=True)) + LAMBDA_INIT)
    lo = lax.broadcasted_iota(jnp.int32, (TQ_B, LANES), 1) < HEAD_DIM

    units = [(qt, h) for qt in range(QT_B) for h in range(B_HEADS)]

    def scores_t(unit):
        qt, h = unit
        cols = slice(h * LANES, (h + 1) * LANES)
        qp = q_ref[0, qt * TQ_B:(qt + 1) * TQ_B, cols].astype(F32)
        q01 = jnp.concatenate([jnp.where(lo, qp, 0.0), jnp.where(lo, 0.0, qp)], axis=0).astype(BF16)
        return _nt_dot(k_ref[0, :, cols], q01)

    ahead = [scores_t(u) for u in units[:QK_AHEAD_B]]
    for n, (qt, h) in enumerate(units):
        cols = slice(h * LANES, (h + 1) * LANES)
        rows = slice(qt * TQ_B, (qt + 1) * TQ_B)
        s_t = ahead.pop(0)
        if n + QK_AHEAD_B < len(units):
            ahead.append(scores_t(units[n + QK_AHEAD_B]))
        if bounded_scores:
            e_t = jnp.exp2(s_t).astype(BF16)
        else:
            m = jnp.max(s_t, axis=0, keepdims=True)
            e_t = jnp.exp2(s_t - m).astype(BF16)
        ol = jnp.dot(vt_ref[0, h * VT_ROWS:(h + 1) * VT_ROWS, :], e_t,
                     preferred_element_type=F32)
        o0 = ol[:LANES, :TQ_B] / ol[LANES:LANES + 1, :TQ_B]
        o1 = ol[:LANES, TQ_B:] / ol[LANES:LANES + 1, TQ_B:]
        o = (o0 - lam * o1).T
        on = o * lax.rsqrt(jnp.mean(o * o, axis=-1, keepdims=True) + EPS)
        on = on * sg_ref[...] * (1.0 - LAMBDA_INIT)
        o_ref[0, rows, cols] = (on * g_ref[0, rows, cols].astype(F32)).astype(BF16)


def _attn_b(lam_params, proj, vt, subln_gain, bounded_scores):
    bsz = proj.shape[0]
    tq = QT_B * TQ_B
    return pl.pallas_call(
        functools.partial(_attn_b_kernel, bounded_scores=bounded_scores),
        grid=(bsz, SEQ // tq),
        in_specs=[pl.BlockSpec((4, HEAD_DIM), lambda b, i: (0, 0)),
                  pl.BlockSpec((1, tq, B_WIDTH), lambda b, i: (b, i, DST["qb"] // B_WIDTH)),
                  pl.BlockSpec((1, SEQ, B_WIDTH), lambda b, i: (b, 0, DST["kb"] // B_WIDTH)),
                  pl.BlockSpec((1, B_HEADS * VT_ROWS, SEQ), lambda b, i: (b, 0, 0)),
                  pl.BlockSpec((1, tq, B_WIDTH), lambda b, i: (b, i, DST["gb"] // B_WIDTH)),
                  pl.BlockSpec((1, LANES), lambda b, i: (0, 0))],
        out_specs=pl.BlockSpec((1, tq, B_WIDTH), lambda b, i: (b, i, 0)),
        out_shape=jax.ShapeDtypeStruct((bsz, SEQ, B_WIDTH), BF16),
        compiler_params=pltpu.CompilerParams(
            dimension_semantics=("parallel", "parallel"), vmem_limit_bytes=VMEM_LIMIT),
        name="attn_b",
    )(lam_params, proj, proj, vt, proj, subln_gain)


def _column_gains(q_norm_a, k_norm_a, q_norm_b, k_norm_b):
    g = jnp.ones((IN_WIDTH,), F32)
    for name, gain, scale in (("qa", q_norm_a, Q_PRESCALE), ("ka", k_norm_a, 1.0),
                              ("qb", q_norm_b, Q_PRESCALE), ("kb", k_norm_b, 1.0)):
        width = _SECTION_WIDTH[name]
        g = g.at[SRC[name]:SRC[name] + width].set(jnp.tile(gain.astype(F32), width // HEAD_DIM) * scale)
    return g.reshape(1, IN_WIDTH)


def kernel(x, c, positions, w_ada, b_ada, norm_gain, w_in, q_norm_a, k_norm_a, sink_a, q_norm_b,
           k_norm_b, lambda_q1, lambda_k1, lambda_q2, lambda_k2, subln_gain, w_out):
    assert w_ada.shape[0] == 1, "single-layer trunk"
    bsz = x.shape[0]
    inv_freq = 1.0 / (ROPE_THETA ** (jnp.arange(0, HEAD_DIM, 2, dtype=F32) / HEAD_DIM))
    ones_blk = jnp.asarray(np.kron(np.eye(MXU_TILE // HEAD_DIM), np.ones((HEAD_DIM, HEAD_DIM))), dtype=BF16)
    colgain = _column_gains(q_norm_a[0], k_norm_a[0], q_norm_b[0], k_norm_b[0])
    lam_params = jnp.concatenate([lambda_q1, lambda_k1, lambda_q2, lambda_k2], axis=0).astype(F32)

    mod3 = _ada(c, w_ada[0], b_ada[0]).reshape(bsz, 3, D_MODEL)
    cos_t, sin_t = _rope_tables(positions, inv_freq)
    proj, vt = _inproj(x, cos_t, sin_t, mod3, norm_gain, w_in[0].astype(BF16), colgain, ones_blk)
    def score_bound(q_gain, k_gain):
        return (HEAD_DIM * Q_PRESCALE * jnp.max(jnp.abs(q_gain.astype(F32)))
                * jnp.max(jnp.abs(k_gain.astype(F32))))

    sink = sink_a[0].astype(F32)
    w_out_bf = w_out[0].astype(BF16)
    bound = jnp.maximum(jnp.maximum(score_bound(q_norm_b[0], k_norm_b[0]),
                                    score_bound(q_norm_a[0], k_norm_a[0])),
                        LOG2E * jnp.max(jnp.abs(sink)))

    def mixers(bounded_scores):
        yb = _attn_b(lam_params, proj, vt, subln_gain, bounded_scores)
        return _attn_a_out(sink, proj, x, yb, w_out_bf, mod3, bounded_scores)

    return lax.cond(bound <= SAFE_LOG2_SCORE, lambda: mixers(True), lambda: mixers(False))
```

```python
import functools
import math

import jax
import jax.numpy as jnp
import numpy as np
from jax import lax
from jax.experimental import pallas as pl
from jax.experimental.pallas import tpu as pltpu

F32 = jnp.float32
BF16 = jnp.bfloat16

D_MODEL = 1024
SEQ = 2048
HEAD_DIM = 64
LANES = 128
A_WIDTH = 512
A_GROUP = 4
MXU_TILE = 256
B_WIDTH = 512
B_HEADS = 4
WINDOW = 128
BLOCK = 128
SPAN = BLOCK + 2 * WINDOW
ROPE_THETA = 10000.0
EPS = 1e-6
LOG2E = math.log2(math.e)
Q_PRESCALE = LOG2E / math.sqrt(HEAD_DIM)
NEG = -1e30
SAFE_LOG2_SCORE = 64.0
LAMBDA_INIT = 0.8 - 0.6 * math.exp(-0.3 * 0)

SRC = dict(qa=0, ka=512, va=640, ga=768, qb=1280, kb=1792, vb=2304, gb=2816)
IN_WIDTH = 3328
DST = dict(qa=0, ga=512, qb=1024, kb=1536, gb=2048, ka=2560, va=3072)
KA_VARIANTS = 4
VA_VARIANTS = 4
PROJ_WIDTH = 3584
VT_ROWS = 144

VMEM_LIMIT = 48 * 1024 * 1024
TM_IN = 1024
NB_A = 8
TQ_B = 512
QT_B = 4
QK_AHEAD_B = 1
OUT_GROUP_ENDS_A = (4, 6, 8)


def _nt_dot(a, b):
    return lax.dot_general(a, b, (((1,), (1,)), ((), ())), preferred_element_type=F32)


def _ada_kernel(c_ref, w_ref, b_ref, o_ref):
    c = c_ref[...]
    sc = c * jax.nn.sigmoid(c)
    o_ref[...] = jnp.dot(sc.astype(BF16), w_ref[...].astype(BF16),
                         preferred_element_type=F32) + b_ref[...]


def _ada(c, w_ada, b_ada):
    bsz = c.shape[0]
    n = w_ada.shape[1]
    tn = D_MODEL
    return pl.pallas_call(
        _ada_kernel,
        grid=(n // tn,),
        in_specs=[pl.BlockSpec((bsz, D_MODEL), lambda j: (0, 0)),
                  pl.BlockSpec((D_MODEL, tn), lambda j: (0, j)),
                  pl.BlockSpec((1, tn), lambda j: (0, j))],
        out_specs=pl.BlockSpec((bsz, tn), lambda j: (0, j)),
        out_shape=jax.ShapeDtypeStruct((bsz, n), F32),
        compiler_params=pltpu.CompilerParams(vmem_limit_bytes=VMEM_LIMIT),
        name="ada",
    )(c, w_ada, b_ada.reshape(1, n))


_SECTION_KIND = dict(qa="qk", ka="ka", va="va", ga="gate", qb="qk", kb="qk", vb="v", gb="gate")
_SECTION_WIDTH = dict(qa=512, ka=128, va=128, ga=512, qb=512, kb=512, vb=512, gb=512)
MM_WIDTH = 1024
EPI_WIDTH = MXU_TILE


def _matmul_chunks():
    pieces = []
    for name in sorted(SRC, key=SRC.get):
        for off in range(0, _SECTION_WIDTH[name], EPI_WIDTH):
            w = min(EPI_WIDTH, _SECTION_WIDTH[name] - off)
            dst = off // LANES if name == "vb" else DST[name] + off
            pieces.append((_SECTION_KIND[name], SRC[name] + off, w, dst))
    chunks = []
    for kind, src, w, dst in pieces:
        if chunks and src + w - chunks[-1][0] <= MM_WIDTH:
            chunks[-1][2].append((kind, src - chunks[-1][0], w, dst))
            chunks[-1][1] = src + w - chunks[-1][0]
        else:
            chunks.append([src, w, [(kind, 0, w, dst)]])
    return tuple((s, w, tuple(p)) for s, w, p in chunks)


_MATMUL_CHUNKS = _matmul_chunks()


def _rope_kernel(pos_ref, invf_ref, cos_ref, sin_ref):
    ang = invf_ref[...] * pos_ref[0].astype(F32)
    cos = jnp.cos(ang)
    sin = jnp.sin(ang)
    cos4 = jnp.concatenate([cos, cos, cos, cos], axis=0)
    sin4 = jnp.concatenate([-sin, sin, -sin, sin], axis=0)
    cos_ref[0] = cos4.T
    sin_ref[0] = sin4.T


def _rope_tables(positions, inv_freq):
    bsz = positions.shape[0]
    nfreq = HEAD_DIM // 2
    return pl.pallas_call(
        _rope_kernel,
        grid=(bsz,),
        in_specs=[pl.BlockSpec((1, 1, SEQ), lambda b: (b, 0, 0)),
                  pl.BlockSpec((nfreq, 1), lambda b: (0, 0))],
        out_specs=[pl.BlockSpec((1, SEQ, LANES), lambda b: (b, 0, 0)),
                   pl.BlockSpec((1, SEQ, LANES), lambda b: (b, 0, 0))],
        out_shape=[jax.ShapeDtypeStruct((bsz, SEQ, LANES), F32)] * 2,
        compiler_params=pltpu.CompilerParams(
            dimension_semantics=("parallel",), vmem_limit_bytes=VMEM_LIMIT),
        name="rope",
    )(positions.reshape(bsz, 1, SEQ), inv_freq.reshape(nfreq, 1))


def _inproj_kernel(x_ref, cos_ref, sin_ref, mod_ref, ng_ref, w_ref, cg_ref, ones_ref,
                   o_ref, vt_ref, h_scr):
    x = x_ref[0]
    ms = jnp.mean(x * x, axis=-1, keepdims=True)
    xn = x * lax.rsqrt(ms + EPS)
    shift = mod_ref[0, 0:1, :]
    scale = mod_ref[0, 1:2, :]
    h = (xn * ng_ref[...]) * (1.0 + scale) + shift
    h_scr[...] = h.astype(BF16)

    cos = cos_ref[0]
    sin = sin_ref[0]
    lane = lax.broadcasted_iota(jnp.int32, cos.shape, 1)
    first_half = (lane & (HEAD_DIM - 1)) < HEAD_DIM // 2
    low_head = lane < HEAD_DIM

    for src0, width, pieces in _MATMUL_CHUNKS:
        pm = jnp.dot(h_scr[...], w_ref[:, src0:src0 + width], preferred_element_type=F32)
        for kind, off, w, dst in pieces:
            src = src0 + off
            p = pm[:, off:off + w]
            if kind == "va":
                p_sw = pltpu.roll(p, HEAD_DIM, 1)
                variants = (jnp.where(low_head, p, 0.0), jnp.where(low_head, 0.0, p_sw),
                            jnp.where(low_head, p_sw, 0.0), jnp.where(low_head, 0.0, p))
                for n, vvar in enumerate(variants):
                    o_ref[0, :, dst + n * LANES:dst + (n + 1) * LANES] = vvar.astype(BF16)
            elif kind == "v":
                p_t = p.T
                for n in range(w // LANES):
                    r0 = (dst + n) * VT_ROWS
                    vt_ref[0, r0:r0 + LANES, :] = p_t[n * LANES:(n + 1) * LANES, :].astype(BF16)
                    vt_ref[0, r0 + LANES:r0 + VT_ROWS, :] = jnp.ones((VT_ROWS - LANES, p_t.shape[1]), BF16)
            elif kind == "gate":
                o_ref[0, :, dst:dst + w] = (p * jax.nn.sigmoid(p)).astype(BF16)
            else:
                ss = jnp.dot((p * p).astype(BF16), ones_ref[0:w, 0:w], preferred_element_type=F32)
                pn = p * lax.rsqrt(ss * (1.0 / HEAD_DIM) + EPS) * cg_ref[:, src:src + w]
                for j in range(w // LANES):
                    t = pn[:, j * LANES:(j + 1) * LANES]
                    rot = jnp.where(first_half, pltpu.roll(t, LANES - HEAD_DIM // 2, 1),
                                    pltpu.roll(t, HEAD_DIM // 2, 1))
                    r = t * cos + rot * sin
                    if kind == "ka":
                        r_sw = pltpu.roll(r, HEAD_DIM, 1)
                        variants = (jnp.where(low_head, r, 0.0), jnp.where(low_head, 0.0, r_sw),
                                    jnp.where(low_head, r_sw, 0.0), jnp.where(low_head, 0.0, r))
                        for n, kvar in enumerate(variants):
                            o_ref[0, :, dst + n * LANES:dst + (n + 1) * LANES] = kvar.astype(BF16)
                    else:
                        o_ref[0, :, dst + j * LANES:dst + (j + 1) * LANES] = r.astype(BF16)


def _inproj(x, cos_t, sin_t, mod3, norm_gain, w_in_bf, colgain, ones_blk):
    bsz = x.shape[0]
    return pl.pallas_call(
        _inproj_kernel,
        grid=(bsz, SEQ // TM_IN),
        in_specs=[pl.BlockSpec((1, TM_IN, D_MODEL), lambda b, i: (b, i, 0)),
                  pl.BlockSpec((1, TM_IN, LANES), lambda b, i: (b, i, 0)),
                  pl.BlockSpec((1, TM_IN, LANES), lambda b, i: (b, i, 0)),
                  pl.BlockSpec((1, 3, D_MODEL), lambda b, i: (b, 0, 0)),
                  pl.BlockSpec((1, D_MODEL), lambda b, i: (0, 0)),
                  pl.BlockSpec((D_MODEL, IN_WIDTH), lambda b, i: (0, 0)),
                  pl.BlockSpec((1, IN_WIDTH), lambda b, i: (0, 0)),
                  pl.BlockSpec((MXU_TILE, MXU_TILE), lambda b, i: (0, 0))],
        out_specs=[pl.BlockSpec((1, TM_IN, PROJ_WIDTH), lambda b, i: (b, i, 0)),
                   pl.BlockSpec((1, B_HEADS * VT_ROWS, TM_IN), lambda b, i: (b, 0, i))],
        out_shape=[jax.ShapeDtypeStruct((bsz, SEQ, PROJ_WIDTH), BF16),
                   jax.ShapeDtypeStruct((bsz, B_HEADS * VT_ROWS, SEQ), BF16)],
        scratch_shapes=[pltpu.VMEM((TM_IN, D_MODEL), BF16)],
        compiler_params=pltpu.CompilerParams(
            dimension_semantics=("parallel", "parallel"), vmem_limit_bytes=VMEM_LIMIT,
            allow_input_fusion=[False] * 5 + [True] + [False] * 2),
        name="inproj",
    )(x, cos_t, sin_t, mod3, norm_gain, w_in_bf, colgain, ones_blk)


def _attn_a_out_kernel(sink_ref, q_ref, ke0_ref, ko0_ref, ke1_ref, ko1_ref, ve0_ref, vo0_ref, ve1_ref,
                       vo1_ref, g_ref, x_ref, yb_ref, wo_ref, mod_ref, o_ref, ya_scr, *, bounded_scores):
    i = pl.program_id(1)
    k_refs = ((ke0_ref, ko0_ref), (ke1_ref, ko1_ref))
    v_refs = ((ve0_ref, vo0_ref), (ve1_ref, vo1_ref))
    row_minus_col = ((lax.broadcasted_iota(jnp.int32, (2 * BLOCK, SPAN), 0) & (BLOCK - 1))
                     - lax.broadcasted_iota(jnp.int32, (2 * BLOCK, SPAN), 1))
    lo_out = lax.broadcasted_iota(jnp.int32, (2 * BLOCK, LANES), 1) < HEAD_DIM
    first_group = lax.broadcasted_iota(jnp.int32, (2 * BLOCK, 1), 0) < BLOCK
    sinks = [[jnp.where(first_group, sink_ref[A_GROUP * kv + parity], sink_ref[A_GROUP * kv + 2 + parity])
              * LOG2E for parity in range(2)] for kv in range(2)]

    def both_parities(refs, start):
        return jnp.concatenate([r[0, pl.ds(start, SPAN), :] for r in refs], axis=0)

    def scores(j):
        blk = i * NB_A + j
        start = pl.multiple_of(jnp.clip(blk * BLOCK - WINDOW, 0, SEQ - SPAN), BLOCK)
        valid = jnp.abs(row_minus_col + (blk * BLOCK - start)) <= WINDOW
        valid = jnp.concatenate([valid, valid], axis=1)
        rows = slice(j * BLOCK, (j + 1) * BLOCK)
        out = []
        for kv in range(2):
            c0 = 2 * kv * LANES
            q2 = jnp.concatenate([q_ref[0, rows, c0:c0 + LANES], q_ref[0, rows, c0 + LANES:c0 + 2 * LANES]],
                                 axis=0)
            s = _nt_dot(q2, both_parities(k_refs[kv], start))
            out.append(jnp.where(valid, s, NEG))
        return start, out

    nxt = scores(0)
    for j in range(NB_A):
        start, s_list = nxt
        if j + 1 < NB_A:
            nxt = scores(j + 1)
        rows = slice(j * BLOCK, (j + 1) * BLOCK)
        for kv in range(2):
            c0 = 2 * kv * LANES
            ps, denoms = [], []
            for parity in range(2):
                s = s_list[kv][:, parity * SPAN:(parity + 1) * SPAN]
                sink = sinks[kv][parity]
                if bounded_scores:
                    p = jnp.exp2(s)
                    denom = jnp.sum(p, axis=-1, keepdims=True) + jnp.exp2(sink)
                else:
                    m = jnp.maximum(jnp.max(s, axis=-1, keepdims=True), sink)
                    p = jnp.exp2(s - m)
                    denom = jnp.sum(p, axis=-1, keepdims=True) + jnp.exp2(sink - m)
                ps.append(p.astype(BF16))
                denoms.append(denom)
            o = jnp.dot(jnp.concatenate(ps, axis=1), both_parities(v_refs[kv], start),
                        preferred_element_type=F32)
            y = o / jnp.where(lo_out, denoms[0], denoms[1])
            for grp in range(2):
                grp_rows = slice(grp * BLOCK, (grp + 1) * BLOCK)
                cols = slice(c0 + grp * LANES, c0 + (grp + 1) * LANES)
                ya_scr[rows, cols] = (y[grp_rows] * g_ref[0, rows, cols].astype(F32)).astype(BF16)
        if j + 1 in OUT_GROUP_ENDS_A:
            group_start = max([0] + [e for e in OUT_GROUP_ENDS_A if e <= j])
            out_rows = slice(group_start * BLOCK, (j + 1) * BLOCK)
            y_out = (jnp.dot(ya_scr[out_rows, :], wo_ref[0:A_WIDTH, :], preferred_element_type=F32)
                     + jnp.dot(yb_ref[0, out_rows, :], wo_ref[A_WIDTH:, :], preferred_element_type=F32))
            o_ref[0, out_rows, :] = x_ref[0, out_rows, :] + mod_ref[0, 2:3, :] * y_out


def _attn_a_out(sink, proj, x, yb, w_out_bf, mod3, bounded_scores):
    bsz = proj.shape[0]
    tq = NB_A * BLOCK

    def kv_spec(col):
        return pl.BlockSpec((1, SEQ, LANES), lambda b, i: (b, 0, col // LANES))

    return pl.pallas_call(
        functools.partial(_attn_a_out_kernel, bounded_scores=bounded_scores),
        grid=(bsz, SEQ // tq),
        in_specs=([pl.BlockSpec(memory_space=pltpu.SMEM),
                   pl.BlockSpec((1, tq, A_WIDTH), lambda b, i: (b, i, DST["qa"] // A_WIDTH))]
                  + [kv_spec(DST["ka"] + n * LANES) for n in range(KA_VARIANTS)]
                  + [kv_spec(DST["va"] + n * LANES) for n in range(VA_VARIANTS)]
                  + [pl.BlockSpec((1, tq, A_WIDTH), lambda b, i: (b, i, DST["ga"] // A_WIDTH)),
                     pl.BlockSpec((1, tq, D_MODEL), lambda b, i: (b, i, 0)),
                     pl.BlockSpec((1, tq, B_WIDTH), lambda b, i: (b, i, 0)),
                     pl.BlockSpec((A_WIDTH + B_WIDTH, D_MODEL), lambda b, i: (0, 0)),
                     pl.BlockSpec((1, 3, D_MODEL), lambda b, i: (b, 0, 0))]),
        out_specs=pl.BlockSpec((1, tq, D_MODEL), lambda b, i: (b, i, 0)),
        out_shape=jax.ShapeDtypeStruct((bsz, SEQ, D_MODEL), F32),
        scratch_shapes=[pltpu.VMEM((tq, A_WIDTH), BF16)],
        compiler_params=pltpu.CompilerParams(
            dimension_semantics=("parallel", "parallel"), vmem_limit_bytes=VMEM_LIMIT,
            allow_input_fusion=[False] * 13 + [True, False]),
        name="attn_a_out",
    )(sink, *([proj] * (2 + KA_VARIANTS + VA_VARIANTS)), x, yb, w_out_bf, mod3)


def _attn_b_kernel(lam_ref, q_ref, k_ref, vt_ref, g_ref, sg_ref, o_ref, *, bounded_scores):
    lq1, lk1, lq2, lk2 = (lam_ref[r:r + 1, :] for r in range(4))
    lam = (jnp.exp(jnp.sum(lq1 * lk1, axis=-1, keepdims=True))
           - jnp.exp(jnp.sum(lq2 * lk2, axis=-1, keepdims=True)) + LAMBDA_INIT)
    lo = lax.broadcasted_iota(jnp.int32, (TQ_B, LANES), 1) < HEAD_DIM

    units = [(qt, h) for qt in range(QT_B) for h in range(B_HEADS)]

    def scores_t(unit):
        qt, h = unit
        cols = slice(h * LANES, (h + 1) * LANES)
        qp = q_ref[0, qt * TQ_B:(qt + 1) * TQ_B, cols].astype(F32)
        q01 = jnp.concatenate([jnp.where(lo, qp, 0.0), jnp.where(lo, 0.0, qp)], axis=0).astype(BF16)
        return _nt_dot(k_ref[0, :, cols], q01)

    ahead = [scores_t(u) for u in units[:QK_AHEAD_B]]
    for n, (qt, h) in enumerate(units):
        cols = slice(h * LANES, (h + 1) * LANES)
        rows = slice(qt * TQ_B, (qt + 1) * TQ_B)
        s_t = ahead.pop(0)
        if n + QK_AHEAD_B < len(units):
            ahead.append(scores_t(units[n + QK_AHEAD_B]))
        if bounded_scores:
            e_t = jnp.exp2(s_t).astype(BF16)
        else:
            m = jnp.max(s_t, axis=0, keepdims=True)
            e_t = jnp.exp2(s_t - m).astype(BF16)
        ol = jnp.dot(vt_ref[0, h * VT_ROWS:(h + 1) * VT_ROWS, :], e_t,
                     preferred_element_type=F32)
        o0 = ol[:LANES, :TQ_B] / ol[LANES:LANES + 1, :TQ_B]
        o1 = ol[:LANES, TQ_B:] / ol[LANES:LANES + 1, TQ_B:]
        o = (o0 - lam * o1).T
        on = o * lax.rsqrt(jnp.mean(o * o, axis=-1, keepdims=True) + EPS)
        on = on * sg_ref[...] * (1.0 - LAMBDA_INIT)
        o_ref[0, rows, cols] = (on * g_ref[0, rows, cols].astype(F32)).astype(BF16)


def _attn_b(lam_params, proj, vt, subln_gain, bounded_scores):
    bsz = proj.shape[0]
    tq = QT_B * TQ_B
    return pl.pallas_call(
        functools.partial(_attn_b_kernel, bounded_scores=bounded_scores),
        grid=(bsz, SEQ // tq),
        in_specs=[pl.BlockSpec((4, HEAD_DIM), lambda b, i: (0, 0)),
                  pl.BlockSpec((1, tq, B_WIDTH), lambda b, i: (b, i, DST["qb"] // B_WIDTH)),
                  pl.BlockSpec((1, SEQ, B_WIDTH), lambda b, i: (b, 0, DST["kb"] // B_WIDTH)),
                  pl.BlockSpec((1, B_HEADS * VT_ROWS, SEQ), lambda b, i: (b, 0, 0)),
                  pl.BlockSpec((1, tq, B_WIDTH), lambda b, i: (b, i, DST["gb"] // B_WIDTH)),
                  pl.BlockSpec((1, LANES), lambda b, i: (0, 0))],
        out_specs=pl.BlockSpec((1, tq, B_WIDTH), lambda b, i: (b, i, 0)),
        out_shape=jax.ShapeDtypeStruct((bsz, SEQ, B_WIDTH), BF16),
        compiler_params=pltpu.CompilerParams(
            dimension_semantics=("parallel", "parallel"), vmem_limit_bytes=VMEM_LIMIT),
        name="attn_b",
    )(lam_params, proj, proj, vt, proj, subln_gain)


def _column_gains(q_norm_a, k_norm_a, q_norm_b, k_norm_b):
    g = jnp.ones((IN_WIDTH,), F32)
    for name, gain, scale in (("qa", q_norm_a, Q_PRESCALE), ("ka", k_norm_a, 1.0),
                              ("qb", q_norm_b, Q_PRESCALE), ("kb", k_norm_b, 1.0)):
        width = _SECTION_WIDTH[name]
        g = g.at[SRC[name]:SRC[name] + width].set(jnp.tile(gain.astype(F32), width // HEAD_DIM) * scale)
    return g.reshape(1, IN_WIDTH)


def kernel(x, c, positions, w_ada, b_ada, norm_gain, w_in, q_norm_a, k_norm_a, sink_a, q_norm_b,
           k_norm_b, lambda_q1, lambda_k1, lambda_q2, lambda_k2, subln_gain, w_out):
    assert w_ada.shape[0] == 1, "single-layer trunk"
    bsz = x.shape[0]
    inv_freq = 1.0 / (ROPE_THETA ** (jnp.arange(0, HEAD_DIM, 2, dtype=F32) / HEAD_DIM))
    ones_blk = jnp.asarray(np.kron(np.eye(MXU_TILE // HEAD_DIM), np.ones((HEAD_DIM, HEAD_DIM))), dtype=BF16)
    colgain = _column_gains(q_norm_a[0], k_norm_a[0], q_norm_b[0], k_norm_b[0])
    lam_params = jnp.concatenate([lambda_q1, lambda_k1, lambda_q2, lambda_k2], axis=0).astype(F32)

    mod3 = _ada(c, w_ada[0], b_ada[0]).reshape(bsz, 3, D_MODEL)
    cos_t, sin_t = _rope_tables(positions, inv_freq)
    proj, vt = _inproj(x, cos_t, sin_t, mod3, norm_gain, w_in[0].astype(BF16), colgain, ones_blk)
    def score_bound(q_gain, k_gain):
        return (HEAD_DIM * Q_PRESCALE * jnp.max(jnp.abs(q_gain.astype(F32)))
                * jnp.max(jnp.abs(k_gain.astype(F32))))

    sink = sink_a[0].astype(F32)
    w_out_bf = w_out[0].astype(BF16)
    bound = jnp.maximum(jnp.maximum(score_bound(q_norm_b[0], k_norm_b[0]),
                                    score_bound(q_norm_a[0], k_norm_a[0])),
                        LOG2E * jnp.max(jnp.abs(sink)))

    def mixers(bounded_scores):
        yb = _attn_b(lam_params, proj, vt, subln_gain, bounded_scores)
        return _attn_a_out(sink, proj, x, yb, w_out_bf, mod3, bounded_scores)

    return lax.cond(bound <= SAFE_LOG2_SCORE, lambda: mixers(True), lambda: mixers(False))
```
